```python
import math
import jax
import jax.numpy as jnp
from jax import lax
import numpy as np

D_MODEL = 1024
BATCH = 16
SEQ = 256
DEPTH = 2
DEC_BATCH = 2
DEC_SEQ = 4096
PAST_LEN = 256

GRID_W = 64
N_BRANCH = 3
NA_HEADS = 8
NA_HEAD_DIM = 64
NA_WIDTH = NA_HEADS * NA_HEAD_DIM
NA_WIN_ROWS = 8
NA_WIN_COLS = 16
FNET_GROUPS = 4
FNET_GROUP_DIM = 128
FNET_WIDTH = FNET_GROUPS * FNET_GROUP_DIM
DIFF_HEADS = 4
DIFF_QK_DIM = 64
DIFF_V_DIM = 2 * DIFF_QK_DIM
DIFF_WIDTH = DIFF_HEADS * DIFF_V_DIM
BRANCH_WIDTH = 512
IN_WIDTH = 4 * NA_WIDTH + 2 * FNET_WIDTH + 4 * DIFF_WIDTH + N_BRANCH * D_MODEL
ROPE_BASE = 10000.0
EPS = 1e-6
Q_BLOCK = 128
NEG_INF = -1e30

kernel_name = "hybrid_diffusion_na_fnet_diffattn_step"

F32 = jnp.float32


def rmsnorm(x, g):
    xf = x.astype(F32)
    y = xf * lax.rsqrt(jnp.mean(xf * xf, axis=-1, keepdims=True) + EPS)
    return (y * g.astype(F32)).astype(x.dtype)


def adaln(cvec, w_mod, b_mod):
    mod = jax.nn.silu(cvec) @ w_mod + b_mod
    mod = mod.reshape(-1, 1, 3 * D_MODEL)
    return jnp.split(mod, 3, axis=-1)


def split_projection(p):
    sizes = [NA_WIDTH] * 4 + [FNET_WIDTH] * 2 + [DIFF_WIDTH] * 4 + [N_BRANCH * D_MODEL]
    offs = [int(o) for o in np.cumsum(sizes)[:-1]]
    return jnp.split(p, offs, axis=-1)


def to_heads(x, n_heads):
    B, T, _ = x.shape
    return x.reshape(B, T, n_heads, -1).transpose(0, 2, 1, 3)


def from_heads(x):
    B, H, T, d = x.shape
    return x.transpose(0, 2, 1, 3).reshape(B, T, H * d)


def diff_qk_heads(x):
    B, T, _ = x.shape
    return x.reshape(B, T, DIFF_HEADS, 2, DIFF_QK_DIM).transpose(0, 2, 1, 3, 4)


def axial_rope_tables(T, dtype):
    t = jnp.arange(T)
    row = (t // GRID_W).astype(F32)
    col = (t % GRID_W).astype(F32)
    half = DIFF_QK_DIM // 2
    inv = ROPE_BASE ** (-jnp.arange(0, half, 2, dtype=F32) / half)
    ang_r = row[:, None] * inv
    ang_c = col[:, None] * inv
    return tuple(a.astype(dtype) for a in (jnp.cos(ang_r), jnp.sin(ang_r), jnp.cos(ang_c), jnp.sin(ang_c)))


def rope_1d(x, cos, sin):
    x1, x2 = jnp.split(x, 2, axis=-1)
    return jnp.concatenate([x1 * cos - x2 * sin, x2 * cos + x1 * sin], axis=-1)


def apply_axial_rope(x, tables):
    cr, sr, cc, sc = (a[:, None, :] for a in tables)
    xr, xc = jnp.split(x, 2, axis=-1)
    return jnp.concatenate([rope_1d(xr, cr, sr), rope_1d(xc, cc, sc)], axis=-1)


def map_query_blocks(fn, q):
    B, H, T = q.shape[:3]
    nb = T // Q_BLOCK
    qb = jnp.moveaxis(q.reshape(B, H, nb, Q_BLOCK, *q.shape[3:]), 2, 0)
    o = lax.map(fn, qb)
    o = jnp.moveaxis(o, 0, 2)
    return o.reshape(B, H, T, o.shape[-1])


def softmax_attention(q, k, v):
    scale = q.shape[-1] ** -0.5

    def block(qb):
        s = jnp.einsum('bhqd,bhkd->bhqk', qb, k).astype(F32) * scale
        p = jax.nn.softmax(s, axis=-1).astype(v.dtype)
        return jnp.einsum('bhqk,bhkd->bhqd', p, v)

    return map_query_blocks(block, q)


def neighbourhood_attention(q, k, v, k_ctx, v_ctx, rpb):
    B, H, T, dh = q.shape
    rows = T // GRID_W
    wr = min(NA_WIN_ROWS, rows)
    wc = NA_WIN_COLS
    scale = dh ** -0.5
    r = jnp.arange(rows)
    row_start = jnp.clip(r - wr // 2, 0, rows - wr)
    key_rows = row_start[:, None] + jnp.arange(wr)[None, :]
    c = jnp.arange(GRID_W)
    col_start = jnp.clip(c - wc // 2, 0, GRID_W - wc)
    col_ok = (c[None, :] >= col_start[:, None]) & (c[None, :] < col_start[:, None] + wc)
    dr_idx = key_rows - r[:, None] + NA_WIN_ROWS - 1
    dc_idx = jnp.clip(c[None, :] - c[:, None] + wc - 1, 0, 2 * wc - 2)
    bias = rpb[:, dr_idx[:, None, :, None], dc_idx[None, :, None, :]].astype(F32)

    qg = q.reshape(B, H, rows, GRID_W, dh)
    kg = k.reshape(B, H, rows, GRID_W, dh)[:, :, key_rows]
    vg = v.reshape(B, H, rows, GRID_W, dh)[:, :, key_rows]
    s_loc = jnp.einsum('bhrqd,bhriwd->bhrqiw', qg, kg).astype(F32) * scale + bias[None]
    s_loc = jnp.where(col_ok[None, None, None, :, None, :], s_loc, NEG_INF)
    s_loc = s_loc.reshape(B, H, rows, GRID_W, wr * GRID_W)
    s_ctx = jnp.einsum('bhrqd,bhnd->bhrqn', qg, k_ctx).astype(F32) * scale
    p = jax.nn.softmax(jnp.concatenate([s_loc, s_ctx], axis=-1), axis=-1).astype(v.dtype)
    p_loc = p[..., :wr * GRID_W].reshape(B, H, rows, GRID_W, wr, GRID_W)
    p_ctx = p[..., wr * GRID_W:]
    o = jnp.einsum('bhrqiw,bhriwd->bhrqd', p_loc, vg) + jnp.einsum('bhrqn,bhnd->bhrqd', p_ctx, v_ctx)
    return o.reshape(B, H, T, dh)


def diff_lambda(lq1, lk1, lq2, lk2, lam_init):
    return (jnp.exp(jnp.sum(lq1.astype(F32) * lk1.astype(F32)))
            - jnp.exp(jnp.sum(lq2.astype(F32) * lk2.astype(F32))) + lam_init)


def diff_attention(q, k, v, lam, subln_g, lam_init):
    scale = DIFF_QK_DIM ** -0.5

    def block(qb):
        s = jnp.einsum('bhqmd,bhkmd->bhmqk', qb, k).astype(F32) * scale
        p = jax.nn.softmax(s, axis=-1)
        a = (p[:, :, 0] - lam * p[:, :, 1]).astype(v.dtype)
        return jnp.einsum('bhqk,bhkd->bhqd', a, v)

    o = map_query_blocks(block, q)
    return rmsnorm(o, subln_g) * (1.0 - lam_init)


def fourier_mix(u, w_f):
    B, T, _ = u.shape
    ug = u.astype(F32).reshape(B, T, FNET_GROUPS, FNET_GROUP_DIM)
    f = jnp.fft.fftn(ug, axes=(1, 3), norm="ortho").real.astype(u.dtype).reshape(B, T, FNET_WIDTH)
    return f @ w_f


def merge_branches(gate_logits, outs, zs, w_branch_l, w_out_l):
    g = jax.nn.sigmoid(gate_logits)
    y = 0.0
    for b in range(N_BRANCH):
        gb = g[..., b * D_MODEL:(b + 1) * D_MODEL]
        y = y + gb * ((outs[b] * jax.nn.silu(zs[b])) @ w_branch_l[b])
    return y @ w_out_l


def mixer_input(x, cvec, norm_g, w_mod, b_mod, w_in):
    shift, scale, gate = adaln(cvec, w_mod, b_mod)
    h = rmsnorm(x, norm_g) * (1.0 + scale) + shift
    return gate, split_projection(h @ w_in)


def setup_inputs(seed: int = 0) -> dict:
    key = jax.random.key(seed)
    ks = jax.random.split(key, 24)
    n = jax.random.normal
    D = D_MODEL
    return {
        "x_prompt": n(ks[0], (BATCH, SEQ, D), F32),
        "x_sample": n(ks[1], (DEC_BATCH, DEC_SEQ, D), F32),
        "cache_na_k": n(ks[2], (DEC_BATCH, DEPTH, NA_HEADS, PAST_LEN, NA_HEAD_DIM), F32),
        "cache_na_v": n(ks[3], (DEC_BATCH, DEPTH, NA_HEADS, PAST_LEN, NA_HEAD_DIM), F32),
        "cache_diff_k": n(ks[4], (DEC_BATCH, DEPTH, DIFF_HEADS, PAST_LEN, 2 * DIFF_QK_DIM), F32),
        "cache_diff_v": n(ks[5], (DEC_BATCH, DEPTH, DIFF_HEADS, PAST_LEN, DIFF_V_DIM), F32),
        "c": n(ks[6], (DEC_BATCH, D), F32),
        "c_ctx": n(ks[7], (D,), F32),
        "norm_g": 1.0 + 0.02 * n(ks[8], (DEPTH, D), F32),
        "w_mod": 0.5 * D ** -0.5 * n(ks[9], (DEPTH, D, 3 * D), F32),
        "b_mod": 0.01 * n(ks[10], (DEPTH, 3 * D), F32),
        "w_in": D ** -0.5 * n(ks[11], (DEPTH, D, IN_WIDTH), F32),
        "na_rpb": 0.1 * n(ks[12], (DEPTH, NA_HEADS, 2 * NA_WIN_ROWS - 1, 2 * NA_WIN_COLS - 1), F32),
        "fnet_w": FNET_WIDTH ** -0.5 * n(ks[13], (DEPTH, FNET_WIDTH, FNET_WIDTH), F32),
        "diff_lq1": 0.1 * n(ks[14], (DEPTH, DIFF_QK_DIM), F32),
        "diff_lk1": 0.1 * n(ks[15], (DEPTH, DIFF_QK_DIM), F32),
        "diff_lq2": 0.1 * n(ks[16], (DEPTH, DIFF_QK_DIM), F32),
        "diff_lk2": 0.1 * n(ks[17], (DEPTH, DIFF_QK_DIM), F32),
        "diff_subln_g": 1.0 + 0.02 * n(ks[18], (DEPTH, DIFF_V_DIM), F32),
        "w_branch": BRANCH_WIDTH ** -0.5 * n(ks[19], (DEPTH, N_BRANCH, BRANCH_WIDTH, D), F32),
        "w_out": D ** -0.5 * n(ks[20], (DEPTH, D, D), F32),
        "final_g": 1.0 + 0.02 * n(ks[21], (D,), F32),
    }


def reference(x_prompt, x_sample, cache_na_k, cache_na_v, cache_diff_k, cache_diff_v, c, c_ctx,
              norm_g, w_mod, b_mod, w_in, na_rpb, fnet_w, diff_lq1, diff_lk1, diff_lq2, diff_lk2,
              diff_subln_g, w_branch, w_out, final_g):
    xp = x_prompt
    Bp, N = xp.shape[0], xp.shape[1]
    na_ks, na_vs, d_ks, d_vs = [], [], [], []
    for l in range(DEPTH):
        lam_init = 0.8 - 0.6 * math.exp(-0.3 * l)
        lam = diff_lambda(diff_lq1[l], diff_lk1[l], diff_lq2[l], diff_lk2[l], lam_init)
        gate, (na_q, na_k, na_v, na_z, f_u, f_z, d_q, d_k, d_v, d_z, g) = mixer_input(
            xp, c_ctx, norm_g[l], w_mod[l], b_mod[l], w_in[l])
        qa, ka, va = to_heads(na_q, NA_HEADS), to_heads(na_k, NA_HEADS), to_heads(na_v, NA_HEADS)
        o_na = from_heads(softmax_attention(qa, ka, va))
        o_f = fourier_mix(f_u, fnet_w[l])
        qd, kd, vd = diff_qk_heads(d_q), diff_qk_heads(d_k), to_heads(d_v, DIFF_HEADS)
        o_d = from_heads(diff_attention(qd, kd, vd, lam, diff_subln_g[l], lam_init))
        y = merge_branches(g, (o_na, o_f, o_d), (na_z, f_z, d_z), w_branch[l], w_out[l])
        xp = xp + gate * y
        na_ks.append(ka)
        na_vs.append(va)
        d_ks.append(kd.reshape(Bp, DIFF_HEADS, N, 2 * DIFF_QK_DIM))
        d_vs.append(vd)
    y_prompt = rmsnorm(xp, final_g)
    new_na_k = jnp.stack(na_ks, axis=1)
    new_na_v = jnp.stack(na_vs, axis=1)
    new_diff_k = jnp.stack(d_ks, axis=1)
    new_diff_v = jnp.stack(d_vs, axis=1)

    xs = x_sample
    Bs, T = xs.shape[0], xs.shape[1]
    Nc = cache_diff_k.shape[3]
    tables = axial_rope_tables(T, xs.dtype)
    for l in range(DEPTH):
        lam_init = 0.8 - 0.6 * math.exp(-0.3 * l)
        lam = diff_lambda(diff_lq1[l], diff_lk1[l], diff_lq2[l], diff_lk2[l], lam_init)
        gate, (na_q, na_k, na_v, na_z, f_u, f_z, d_q, d_k, d_v, d_z, g) = mixer_input(
            xs, c, norm_g[l], w_mod[l], b_mod[l], w_in[l])
        qa, ka, va = to_heads(na_q, NA_HEADS), to_heads(na_k, NA_HEADS), to_heads(na_v, NA_HEADS)
        o_na = from_heads(neighbourhood_attention(qa, ka, va, cache_na_k[:, l], cache_na_v[:, l], na_rpb[l]))
        o_f = fourier_mix(f_u, fnet_w[l])
        qd = apply_axial_rope(diff_qk_heads(d_q), tables)
        kd = apply_axial_rope(diff_qk_heads(d_k), tables)
        vd = to_heads(d_v, DIFF_HEADS)
        k_ctx = cache_diff_k[:, l].reshape(Bs, DIFF_HEADS, Nc, 2, DIFF_QK_DIM)
        k_all = jnp.concatenate([kd, k_ctx], axis=2)
        v_all = jnp.concatenate([vd, cache_diff_v[:, l]], axis=2)
        o_d = from_heads(diff_attention(qd, k_all, v_all, lam, diff_subln_g[l], lam_init))
        y = merge_branches(g, (o_na, o_f, o_d), (na_z, f_z, d_z), w_branch[l], w_out[l])
        xs = xs + gate * y
    y_sample = rmsnorm(xs, final_g)
    return (y_prompt, y_sample, new_na_k, new_na_v, new_diff_k, new_diff_v)
```

```python
import functools
import math

import jax
import jax.numpy as jnp
import numpy as np
from jax import lax
from jax.experimental import pallas as pl
from jax.experimental.pallas import tpu as pltpu

F32 = jnp.float32
BF16 = jnp.bfloat16

D_MODEL = 1024
DEPTH = 2
GRID_W = 64
NA_HEADS = 8
NA_HEAD_DIM = 64
NA_WIN_ROWS = 8
NA_WIN_COLS = 16
FNET_GROUPS = 4
FNET_GROUP_DIM = 128
DIFF_HEADS = 4
DIFF_QK_DIM = 64
DIFF_V_DIM = 128
BRANCH_WIDTH = 512
IN_WIDTH = 8192
ROPE_BASE = 10000.0
EPS = 1e-6
NEG_INF = -1e30

CB_NA_Q, CB_NA_K, CB_NA_V, CB_NA_Z, CB_F_U, CB_F_Z, CB_D_Q, CB_D_K, CB_D_V, CB_D_Z, CB_G = range(11)

LANES = 128
VMEM_LIMIT = 56 * 1024 * 1024

PROJ_TN = 1024
NA_QROWS = 8
NA_KROWS = 16
DIFF_TQ = 256
FFT_RADIX = 8


def _cparams(sem):
    return pltpu.CompilerParams(dimension_semantics=sem, vmem_limit_bytes=VMEM_LIMIT)


def _dot(a, b):
    return jnp.dot(a, b, preferred_element_type=F32)


def _dot_nt(a, b):
    return lax.dot_general(a, b, (((1,), (1,)), ((), ())), preferred_element_type=F32)


def _half_masks(dtype):
    lane = lax.broadcasted_iota(jnp.int32, (1, LANES), 1)
    lo = (lane < LANES // 2).astype(dtype)
    return lo, (1 - lo).astype(dtype)


def _mod_kernel(c_ref, w_ref, b_ref, o_ref):
    c = c_ref[...]
    s = c * jax.nn.sigmoid(c)
    o_ref[...] = _dot(s.astype(BF16), w_ref[...]) + b_ref[...]


def _modulation(cvecs, w_mod, b_mod):
    nt = 3 * D_MODEL // PROJ_TN
    return pl.pallas_call(
        _mod_kernel,
        grid=(DEPTH, nt),
        in_specs=[
            pl.BlockSpec((8, D_MODEL), lambda l, n: (0, 0)),
            pl.BlockSpec((None, D_MODEL, PROJ_TN), lambda l, n: (l, 0, n)),
            pl.BlockSpec((None, 1, PROJ_TN), lambda l, n: (l, 0, n)),
        ],
        out_specs=pl.BlockSpec((None, 8, PROJ_TN), lambda l, n: (l, 0, n)),
        out_shape=jax.ShapeDtypeStruct((DEPTH, 8, 3 * D_MODEL), F32),
        compiler_params=_cparams(("arbitrary", "arbitrary")),
        name="adaln_mod",
    )(cvecs, w_mod, b_mod.reshape(DEPTH, 1, 3 * D_MODEL))


def _rope_swap(x):
    w = x.shape[-1]
    lane = lax.broadcasted_iota(jnp.int32, (1, w), 1)
    first = (lane % 32) < 16
    return jnp.where(first, pltpu.roll(x, w - 16, 1), pltpu.roll(x, 16, 1))


def _proj_kernel(*refs, rope, kv_out, tm):
    it = iter(refs)
    x_ref, shift_ref, scale_ref, ng_ref, w_ref, cs_ref = (next(it) for _ in range(6))
    cos_ref = sin_ref = None
    if rope:
        cos_ref, sin_ref = next(it), next(it)
    p_ref = next(it)
    if kv_out:
        nak_ref, nav_ref, dk_ref, dv_ref = (next(it) for _ in range(4))
    h_scr = next(it)
    j = pl.program_id(1)

    @pl.when(j == 0)
    def _():
        x = x_ref[...]
        y = x * lax.rsqrt(jnp.mean(x * x, axis=-1, keepdims=True) + EPS) * ng_ref[...]
        h_scr[...] = (y * (1.0 + scale_ref[...]) + shift_ref[...]).astype(BF16)

    acc = _dot(h_scr[...], w_ref[...])

    if kv_out:
        nb = tm // 256

        def heads_out(ref, cols, width):
            for bb in range(nb):
                for h in range(512 // width):
                    ref[bb, h] = acc[bb * 256:(bb + 1) * 256, cols + h * width:cols + (h + 1) * width]

        @pl.when(j == 0)
        def _():
            heads_out(nak_ref, 512, NA_HEAD_DIM)

        @pl.when(j == 1)
        def _():
            heads_out(nav_ref, 0, NA_HEAD_DIM)

        @pl.when(j == 3)
        def _():
            heads_out(dk_ref, 512, 2 * DIFF_QK_DIM)

        @pl.when(j == 4)
        def _():
            heads_out(dv_ref, 0, DIFF_V_DIM)

    if rope:
        @pl.when(j == 3)
        def _():
            c = jnp.tile(cos_ref[...], (1, PROJ_TN // LANES))
            s = jnp.tile(sin_ref[...], (1, PROJ_TN // LANES))
            r = acc * c + _rope_swap(acc) * s
            p_ref[...] = (r * cs_ref[...]).astype(BF16)

        @pl.when(j != 3)
        def _():
            p_ref[...] = (acc * cs_ref[...]).astype(BF16)
    else:
        p_ref[...] = (acc * cs_ref[...]).astype(BF16)


def _projection(x2d, mod_l, ng, w_in_l, colscale, *, tm, mod_row, rope_tabs=None, kv_out=False, seq=None):
    m = x2d.shape[0]
    nm, nn = m // tm, IN_WIDTH // PROJ_TN
    rope = rope_tabs is not None
    in_specs = [
        pl.BlockSpec((tm, D_MODEL), lambda i, j: (i, 0)),
        pl.BlockSpec((None, 1, D_MODEL), lambda i, j: (mod_row(i) * 3 + 0, 0, 0)),
        pl.BlockSpec((None, 1, D_MODEL), lambda i, j: (mod_row(i) * 3 + 1, 0, 0)),
        pl.BlockSpec((1, D_MODEL), lambda i, j: (0, 0)),
        pl.BlockSpec((D_MODEL, PROJ_TN), lambda i, j: (0, j)),
        pl.BlockSpec((None, 1, PROJ_TN), lambda i, j: (j, 0, 0)),
    ]
    args = [x2d, mod_l, mod_l, ng, w_in_l, colscale]
    if rope:
        tiles_per_seq = seq // tm
        in_specs += [pl.BlockSpec((tm, LANES), lambda i, j: (i % tiles_per_seq, 0))] * 2
        args += list(rope_tabs)
    out_specs = [pl.BlockSpec((tm, PROJ_TN), lambda i, j: (i, j))]
    out_shape = [jax.ShapeDtypeStruct((m, IN_WIDTH), BF16)]
    if kv_out:
        nb, b = tm // 256, m // 256
        for heads, dh in ((NA_HEADS, NA_HEAD_DIM), (NA_HEADS, NA_HEAD_DIM),
                          (DIFF_HEADS, 2 * DIFF_QK_DIM), (DIFF_HEADS, DIFF_V_DIM)):
            out_specs.append(pl.BlockSpec((nb, heads, 256, dh), lambda i, j: (i, 0, 0, 0)))
            out_shape.append(jax.ShapeDtypeStruct((b, heads, 256, dh), F32))
    return pl.pallas_call(
        functools.partial(_proj_kernel, rope=rope, kv_out=kv_out, tm=tm),
        grid=(nm, nn),
        in_specs=in_specs,
        out_specs=out_specs,
        out_shape=out_shape,
        scratch_shapes=[pltpu.VMEM((tm, D_MODEL), BF16)],
        compiler_params=_cparams(("arbitrary", "arbitrary")),
        name="in_proj_kv" if kv_out else "in_proj_rope",
    )(*args)


def _diff_lambda(lq1_ref, lk1_ref, lq2_ref, lk2_ref, lam_init):
    a = jnp.sum(lq1_ref[...] * lk1_ref[...], axis=-1, keepdims=True)
    b = jnp.sum(lq2_ref[...] * lk2_ref[...], axis=-1, keepdims=True)
    return jnp.exp(a) - jnp.exp(b) + lam_init


def _subln(o, g, lam_init):
    y = o * lax.rsqrt(jnp.mean(o * o, axis=-1, keepdims=True) + EPS)
    return (y * g) * (1.0 - lam_init)


def _prompt_attn_kernel(qa_ref, ka_ref, va_ref, qd_ref, kd_ref, vd_ref,
                        lq1_ref, lk1_ref, lq2_ref, lk2_ref, sg_ref, ona_ref, od_ref, *, lam_init):
    n = qa_ref.shape[0]
    mlo, mhi = _half_masks(BF16)
    for c in range(NA_HEADS // 2):
        sl = slice(c * LANES, (c + 1) * LANES)
        q, k, v = qa_ref[:, sl], ka_ref[:, sl], va_ref[:, sl]
        s = _dot_nt(jnp.concatenate([q * mlo, q * mhi], axis=0), k)
        e = jnp.exp(s - jnp.max(s, axis=-1, keepdims=True))
        p = (e / jnp.sum(e, axis=-1, keepdims=True)).astype(BF16)
        pcat = jnp.concatenate([p[:n], p[n:]], axis=1)
        vbd = jnp.concatenate([v * mlo, v * mhi], axis=0)
        ona_ref[:, sl] = _dot(pcat, vbd).astype(BF16)
    lam = _diff_lambda(lq1_ref, lk1_ref, lq2_ref, lk2_ref, lam_init)
    for h in range(DIFF_HEADS):
        sl = slice(h * LANES, (h + 1) * LANES)
        q, k, v = qd_ref[:, sl], kd_ref[:, sl], vd_ref[:, sl]
        s = _dot_nt(jnp.concatenate([q * mlo, q * mhi], axis=0), k)
        e = jnp.exp(s - jnp.max(s, axis=-1, keepdims=True))
        p = e / jnp.sum(e, axis=-1, keepdims=True)
        a = (p[:n] - lam * p[n:]).astype(BF16)
        od_ref[:, sl] = _subln(_dot(a, v), sg_ref[...], lam_init).astype(BF16)


def _prompt_attention(p, lam_params, sg, lam_init, *, batch, seq):
    def cb(k):
        return pl.BlockSpec((seq, BRANCH_WIDTH), lambda b: (b, k))
    small = pl.BlockSpec((1, DIFF_QK_DIM), lambda b: (0, 0))
    out = pl.BlockSpec((seq, BRANCH_WIDTH), lambda b: (b, 0))
    return pl.pallas_call(
        functools.partial(_prompt_attn_kernel, lam_init=lam_init),
        grid=(batch,),
        in_specs=[cb(CB_NA_Q), cb(CB_NA_K), cb(CB_NA_V), cb(CB_D_Q), cb(CB_D_K), cb(CB_D_V),
                  small, small, small, small, pl.BlockSpec((1, DIFF_V_DIM), lambda b: (0, 0))],
        out_specs=[out, out],
        out_shape=[jax.ShapeDtypeStruct((batch * seq, BRANCH_WIDTH), BF16)] * 2,
        compiler_params=_cparams(("arbitrary",)),
        name="prompt_attn",
    )(p, p, p, p, p, p, *lam_params, sg)


def _na_variant_geometry(variant, rows):
    jv = (0, 1, rows // NA_QROWS - 1)[variant]
    kstart = min(max(NA_QROWS * jv - NA_WIN_ROWS // 2, 0), rows - NA_KROWS)
    return jv, kstart


def _na_build_bias(rpb_ref, bias_scr, layer, pair, rows):
    cq = lax.broadcasted_iota(jnp.int32, (GRID_W, GRID_W), 0)
    ck = lax.broadcasted_iota(jnp.int32, (GRID_W, GRID_W), 1)
    col_start = jnp.clip(cq - NA_WIN_COLS // 2, 0, GRID_W - NA_WIN_COLS)
    col_ok = (ck >= col_start) & (ck < col_start + NA_WIN_COLS)
    dc = jnp.clip(ck - cq + NA_WIN_COLS - 1, 0, 2 * NA_WIN_COLS - 2)
    n_dr, n_dc = 2 * NA_WIN_ROWS - 1, 2 * NA_WIN_COLS - 1
    neg = jnp.full((GRID_W, GRID_W), NEG_INF, F32)
    for hh in range(2):
        base = ((layer * NA_HEADS + 2 * pair + hh) * n_dr) * n_dc
        tabs = []
        for a in range(n_dr):
            t = jnp.zeros((GRID_W, GRID_W), F32)
            for b in range(n_dc):
                t = jnp.where(dc == b, rpb_ref[base + a * n_dc + b], t)
            tabs.append(jnp.where(col_ok, t, NEG_INF))
        for variant in range(3):
            jv, kstart = _na_variant_geometry(variant, rows)
            for i in range(NA_QROWS):
                r = NA_QROWS * jv + i
                rs = min(max(r - NA_WIN_ROWS // 2, 0), rows - NA_WIN_ROWS)
                blocks = []
                for u in range(NA_KROWS):
                    kr = kstart + u
                    blocks.append(tabs[kr - r + NA_WIN_ROWS - 1] if rs <= kr < rs + NA_WIN_ROWS else neg)
                bias_scr[hh, variant, i * GRID_W:(i + 1) * GRID_W, :] = jnp.concatenate(blocks, axis=1)


def _na_kernel(rpb_ref, q_ref, k_ref, v_ref, kc_ref, vc_ref, o_ref, bias_scr, *, layer, rows):
    pair, b, j = pl.program_id(0), pl.program_id(1), pl.program_id(2)
    nj = rows // NA_QROWS
    nq, nk = NA_QROWS * GRID_W, NA_KROWS * GRID_W

    @pl.when((b == 0) & (j == 0))
    def _():
        _na_build_bias(rpb_ref, bias_scr, layer, pair, rows)

    variant = jnp.where(j == 0, 0, jnp.where(j == nj - 1, 2, 1))
    kstart = jnp.clip(NA_QROWS * j - NA_WIN_ROWS // 2, 0, rows - NA_KROWS)
    k0 = pl.multiple_of(kstart * GRID_W, 256)
    mlo, mhi = _half_masks(BF16)
    q = q_ref[...]
    qq = jnp.concatenate([q * mlo, q * mhi], axis=0)
    kl, vl = k_ref[pl.ds(k0, nk), :], v_ref[pl.ds(k0, nk), :]
    kc, vc = kc_ref[...], vc_ref[...]
    bias = jnp.concatenate([bias_scr[0, variant], bias_scr[1, variant]], axis=0)
    s_loc = _dot_nt(qq, kl) + bias
    s_ctx = _dot_nt(qq, kc)
    m = jnp.maximum(jnp.max(s_loc, axis=-1, keepdims=True), jnp.max(s_ctx, axis=-1, keepdims=True))
    e_loc, e_ctx = jnp.exp(s_loc - m), jnp.exp(s_ctx - m)
    l = jnp.sum(e_loc, axis=-1, keepdims=True) + jnp.sum(e_ctx, axis=-1, keepdims=True)
    e_loc, e_ctx = e_loc.astype(BF16), e_ctx.astype(BF16)
    pcat = jnp.concatenate([e_loc[:nq], e_ctx[:nq], e_loc[nq:], e_ctx[nq:]], axis=1)
    vbd = jnp.concatenate([vl * mlo, vc * mlo, vl * mhi, vc * mhi], axis=0)
    o = _dot(pcat, vbd)
    lane = lax.broadcasted_iota(jnp.int32, (1, LANES), 1)
    inv = 1.0 / l
    o_ref[...] = (o * jnp.where(lane < LANES // 2, inv[:nq], inv[nq:])).astype(BF16)


def _neighbourhood_attention(p, kc, vc, rpb_flat, layer, *, batch, seq):
    rows = seq // GRID_W
    nj = rows // NA_QROWS
    nq = NA_QROWS * GRID_W
    n_ctx = kc.shape[2]
    cpb = BRANCH_WIDTH // LANES
    return pl.pallas_call(
        functools.partial(_na_kernel, layer=layer, rows=rows),
        grid=(NA_HEADS // 2, batch, nj),
        in_specs=[
            pl.BlockSpec(memory_space=pltpu.SMEM),
            pl.BlockSpec((nq, LANES), lambda c, b, j: (b * nj + j, CB_NA_Q * cpb + c)),
            pl.BlockSpec((seq, LANES), lambda c, b, j: (b, CB_NA_K * cpb + c)),
            pl.BlockSpec((seq, LANES), lambda c, b, j: (b, CB_NA_V * cpb + c)),
            pl.BlockSpec((None, None, n_ctx, LANES), lambda c, b, j: (b, layer, 0, c)),
            pl.BlockSpec((None, None, n_ctx, LANES), lambda c, b, j: (b, layer, 0, c)),
        ],
        out_specs=pl.BlockSpec((nq, LANES), lambda c, b, j: (b * nj + j, c)),
        out_shape=jax.ShapeDtypeStruct((batch * seq, BRANCH_WIDTH), BF16),
        scratch_shapes=[pltpu.VMEM((2, 3, nq, NA_KROWS * GRID_W), F32)],
        compiler_params=_cparams(("arbitrary", "arbitrary", "arbitrary")),
        name="na_attn",
    )(rpb_flat, p, p, p, kc, vc)


def _diff_kernel(q_ref, k_ref, v_ref, kc_ref, vc_ref, lq1_ref, lk1_ref, lq2_ref, lk2_ref, sg_ref,
                 o_ref, kall_scr, vbd_scr, *, lam_init, seq):
    qi = pl.program_id(2)
    n_all = kall_scr.shape[0]
    tq = q_ref.shape[0]
    mlo, mhi = _half_masks(BF16)

    @pl.when(qi == 0)
    def _():
        kall_scr[0:seq, :] = k_ref[...]
        kall_scr[seq:n_all, :] = kc_ref[...].astype(BF16)
        v_all = jnp.concatenate([v_ref[...], vc_ref[...].astype(BF16)], axis=0)
        z = jnp.zeros_like(v_all)
        vbd_scr[0:n_all, :] = jnp.concatenate([v_all, z], axis=1)
        vbd_scr[n_all:2 * n_all, :] = jnp.concatenate([z, v_all], axis=1)

    q = q_ref[...]
    s = _dot_nt(jnp.concatenate([q * mlo, q * mhi], axis=0), kall_scr[...])
    e = jnp.exp(s - jnp.max(s, axis=-1, keepdims=True))
    inv = 1.0 / jnp.sum(e, axis=-1, keepdims=True)
    e = e.astype(BF16)
    o12 = _dot(jnp.concatenate([e[:tq], e[tq:]], axis=1), vbd_scr[...])
    lam = _diff_lambda(lq1_ref, lk1_ref, lq2_ref, lk2_ref, lam_init)
    o = o12[:, :DIFF_V_DIM] * inv[:tq] - lam * (o12[:, DIFF_V_DIM:] * inv[tq:])
    o_ref[...] = _subln(o, sg_ref[...], lam_init).astype(BF16)


def _diff_attention(p, kc, vc, lam_params, sg, lam_init, layer, *, batch, seq):
    nq = seq // DIFF_TQ
    n_ctx = kc.shape[3]
    n_all = seq + n_ctx
    cpb = BRANCH_WIDTH // LANES
    small = pl.BlockSpec((1, DIFF_QK_DIM), lambda b, h, i: (0, 0))
    ctx = pl.BlockSpec((None, None, None, n_ctx, LANES), lambda b, h, i: (b, layer, h, 0, 0))
    return pl.pallas_call(
        functools.partial(_diff_kernel, lam_init=lam_init, seq=seq),
        grid=(batch, DIFF_HEADS, nq),
        in_specs=[
            pl.BlockSpec((DIFF_TQ, LANES), lambda b, h, i: (b * nq + i, CB_D_Q * cpb + h)),
            pl.BlockSpec((seq, LANES), lambda b, h, i: (b, CB_D_K * cpb + h)),
            pl.BlockSpec((seq, LANES), lambda b, h, i: (b, CB_D_V * cpb + h)),
            ctx, ctx, small, small, small, small,
            pl.BlockSpec((1, DIFF_V_DIM), lambda b, h, i: (0, 0)),
        ],
        out_specs=pl.BlockSpec((DIFF_TQ, LANES), lambda b, h, i: (b * nq + i, h)),
        out_shape=jax.ShapeDtypeStruct((batch * seq, BRANCH_WIDTH), BF16),
        scratch_shapes=[pltpu.VMEM((n_all, LANES), BF16), pltpu.VMEM((2 * n_all, 2 * LANES), BF16)],
        compiler_params=_cparams(("arbitrary", "arbitrary", "arbitrary")),
        name="diff_attn",
    )(p, p, p, kc, vc, *lam_params, sg)


def _fnet_tail(x3, cs_ref, wf_ref, n):
    f = _dot(x3[:n].astype(BF16), cs_ref[0:BRANCH_WIDTH, :]) + _dot(x3[n:].astype(BF16), cs_ref[BRANCH_WIDTH:, :])
    return _dot(f.astype(BF16), wf_ref[...])


def _fnet_small_kernel(u_ref, dft_ref, cs_ref, wf_ref, o_ref):
    n = u_ref.shape[0]
    x3 = _dot(dft_ref[...], u_ref[...])
    o_ref[...] = _fnet_tail(x3, cs_ref, wf_ref, n).astype(BF16)


def _fnet_small(p, dft, cs, wf, *, batch, seq):
    return pl.pallas_call(
        _fnet_small_kernel,
        grid=(batch,),
        in_specs=[
            pl.BlockSpec((seq, BRANCH_WIDTH), lambda b: (b, CB_F_U)),
            pl.BlockSpec(dft.shape, lambda b: (0, 0)),
            pl.BlockSpec(cs.shape, lambda b: (0, 0)),
            pl.BlockSpec(wf.shape, lambda b: (0, 0)),
        ],
        out_specs=pl.BlockSpec((seq, BRANCH_WIDTH), lambda b: (b, 0)),
        out_shape=jax.ShapeDtypeStruct((batch * seq, BRANCH_WIDTH), BF16),
        compiler_params=_cparams(("arbitrary",)),
        name="fnet_prompt",
    )(p, dft, cs, wf)


FFT_CHUNK = 16


def _fft_stage1(u_ref, twc_ref, tws_ref, s_scr, n2):
    h = math.sqrt(0.5)
    rep = BRANCH_WIDTH // LANES

    def body(ci, carry):
        r0 = pl.multiple_of(ci * FFT_CHUNK, FFT_CHUNK)
        a = [u_ref[pl.ds(r * n2 + r0, FFT_CHUNK), :].astype(F32) for r in range(FFT_RADIX)]
        s04, d04, s26, d26 = a[0] + a[4], a[0] - a[4], a[2] + a[6], a[2] - a[6]
        s15, d15, s37, d37 = a[1] + a[5], a[1] - a[5], a[3] + a[7], a[3] - a[7]
        ee, oo = s04 + s26, s15 + s37
        dm, dp = h * (d15 - d37), h * (d15 + d37)
        zero = jnp.zeros_like(ee)
        re = [ee + oo, d04 + dm, s04 - s26, d04 - dm, ee - oo]
        im = [zero, dp + d26, s15 - s37, dp - d26, zero]
        for k1 in range(FFT_RADIX):
            kk = k1 if k1 <= FFT_RADIX // 2 else FFT_RADIX - k1
            ar, ai = re[kk], (im[kk] if k1 <= FFT_RADIX // 2 else -im[kk])
            ct = jnp.tile(twc_ref[k1, pl.ds(r0, FFT_CHUNK), :], (1, rep))
            st = jnp.tile(tws_ref[k1, pl.ds(r0, FFT_CHUNK), :], (1, rep))
            s_scr[k1, pl.ds(r0, FFT_CHUNK), :] = (ar * ct - ai * st).astype(BF16)
            s_scr[k1, pl.ds(n2 + r0, FFT_CHUNK), :] = (ar * st + ai * ct).astype(BF16)
        return carry

    lax.fori_loop(0, n2 // FFT_CHUNK, body, 0)


def _fnet_large_kernel(u_ref, twc_ref, tws_ref, d3_ref, cs_ref, wf_ref, o_ref, s_scr):
    k1 = pl.program_id(1)
    n2 = s_scr.shape[1] // 2

    @pl.when(k1 == 0)
    def _():
        _fft_stage1(u_ref, twc_ref, tws_ref, s_scr, n2)

    x3 = _dot(d3_ref[...], s_scr[k1])
    o_ref[...] = _fnet_tail(x3, cs_ref, wf_ref, n2).astype(BF16)


def _fnet_large(p, twc, tws, d3, cs, wf, *, batch, seq):
    n2 = seq // FFT_RADIX
    out = pl.pallas_call(
        _fnet_large_kernel,
        grid=(batch, FFT_RADIX),
        in_specs=[
            pl.BlockSpec((seq, BRANCH_WIDTH), lambda b, k: (b, CB_F_U)),
            pl.BlockSpec(twc.shape, lambda b, k: (0, 0, 0)),
            pl.BlockSpec(tws.shape, lambda b, k: (0, 0, 0)),
            pl.BlockSpec(d3.shape, lambda b, k: (0, 0)),
            pl.BlockSpec(cs.shape, lambda b, k: (0, 0)),
            pl.BlockSpec(wf.shape, lambda b, k: (0, 0)),
        ],
        out_specs=pl.BlockSpec((None, n2, BRANCH_WIDTH), lambda b, k: (b, 0, k)),
        out_shape=jax.ShapeDtypeStruct((batch, n2, FFT_RADIX * BRANCH_WIDTH), BF16),
        scratch_shapes=[pltpu.VMEM((FFT_RADIX, 2 * n2, BRANCH_WIDTH), BF16)],
        compiler_params=_cparams(("arbitrary", "arbitrary")),
        name="fnet_sample",
    )(p, twc, tws, d3, cs, wf)
    return out.reshape(batch * seq, BRANCH_WIDTH)


def _dft_constants(seq):
    gd = FNET_GROUP_DIM
    norm = 1.0 / math.sqrt(seq * gd)
    ang = 2.0 * np.pi * (np.outer(np.arange(gd), np.arange(gd)) % gd) / gd
    eye = np.eye(FNET_GROUPS)
    cs = np.concatenate([np.kron(eye, np.cos(ang)), -np.kron(eye, np.sin(ang))], axis=0) * norm
    if seq % (FFT_RADIX * FFT_CHUNK) != 0 or seq < 2048:
        ang_t = 2.0 * np.pi * (np.outer(np.arange(seq), np.arange(seq)) % seq) / seq
        dft = np.concatenate([np.cos(ang_t), np.sin(ang_t)], axis=0)
        return dict(dft=jnp.asarray(dft, F32).astype(BF16), cs=jnp.asarray(cs, F32).astype(BF16))
    n2 = seq // FFT_RADIX
    ang2 = 2.0 * np.pi * (np.outer(np.arange(n2), np.arange(n2)) % n2) / n2
    c2, s2 = np.cos(ang2), np.sin(ang2)
    d3 = np.block([[c2, -s2], [s2, c2]])
    angw = 2.0 * np.pi * np.outer(np.arange(FFT_RADIX), np.arange(n2)) / seq
    twc = np.repeat(np.cos(angw)[:, :, None], LANES, axis=2)
    tws = np.repeat(np.sin(angw)[:, :, None], LANES, axis=2)
    return dict(d3=jnp.asarray(d3, F32).astype(BF16), cs=jnp.asarray(cs, F32).astype(BF16),
                twc=jnp.asarray(twc, F32), tws=jnp.asarray(tws, F32))


def _merge_kernel(x_ref, ona_ref, of_ref, od_ref, zna_ref, zf_ref, zd_ref, g0_ref, g1_ref, g2_ref,
                  gate_ref, wb_ref, wo_ref, fg_ref, o_ref, *, final):
    y = None
    for b, (o_b, z_b, g_b) in enumerate(((ona_ref, zna_ref, g0_ref), (of_ref, zf_ref, g1_ref), (od_ref, zd_ref, g2_ref))):
        z = z_b[...].astype(F32)
        u = (o_b[...].astype(F32) * (z * jax.nn.sigmoid(z))).astype(BF16)
        t = jax.nn.sigmoid(g_b[...].astype(F32)) * _dot(u, wb_ref[b])
        y = t if y is None else y + t
    xn = x_ref[...] + gate_ref[...] * _dot(y.astype(BF16), wo_ref[...])
    if final:
        xn = xn * lax.rsqrt(jnp.mean(xn * xn, axis=-1, keepdims=True) + EPS) * fg_ref[...]
    o_ref[...] = xn


def _merge(x2d, o_na, o_f, o_d, p, mod_l, wb, wo, fg, *, tm, mod_row, final):
    m = x2d.shape[0]

    def cb(k):
        return pl.BlockSpec((tm, BRANCH_WIDTH), lambda i: (i, k))

    def gb(k):
        return pl.BlockSpec((tm, D_MODEL), lambda i: (i, CB_G // 2 + k))

    br = pl.BlockSpec((tm, BRANCH_WIDTH), lambda i: (i, 0))
    return pl.pallas_call(
        functools.partial(_merge_kernel, final=final),
        grid=(m // tm,),
        in_specs=[
            pl.BlockSpec((tm, D_MODEL), lambda i: (i, 0)), br, br, br,
            cb(CB_NA_Z), cb(CB_F_Z), cb(CB_D_Z), gb(0), gb(1), gb(2),
            pl.BlockSpec((None, 1, D_MODEL), lambda i: (mod_row(i) * 3 + 2, 0, 0)),
            pl.BlockSpec(wb.shape, lambda i: (0, 0, 0)),
            pl.BlockSpec(wo.shape, lambda i: (0, 0)),
            pl.BlockSpec((1, D_MODEL), lambda i: (0, 0)),
        ],
        out_specs=pl.BlockSpec((tm, D_MODEL), lambda i: (i, 0)),
        out_shape=jax.ShapeDtypeStruct((m, D_MODEL), F32),
        compiler_params=_cparams(("arbitrary",)),
        name="merge_final" if final else "merge",
    )(x2d, o_na, o_f, o_d, p, p, p, p, p, p, mod_l, wb, wo, fg)


def _rope_tables(seq):
    t = jnp.arange(seq)
    row = (t // GRID_W).astype(F32)
    col = (t % GRID_W).astype(F32)
    half = DIFF_QK_DIM // 2
    inv = ROPE_BASE ** (-jnp.arange(0, half, 2, dtype=F32) / half)
    ar, ac = row[:, None] * inv, col[:, None] * inv
    cr, sr, cc, sc = jnp.cos(ar), jnp.sin(ar), jnp.cos(ac), jnp.sin(ac)
    cos = jnp.concatenate([cr, cr, cc, cc] * 2, axis=-1)
    sin = jnp.concatenate([-sr, sr, -sc, sc] * 2, axis=-1)
    return cos, sin


def _column_scales():
    s = np.ones((IN_WIDTH,), np.float32)
    s[CB_NA_Q * 512:(CB_NA_Q + 1) * 512] = NA_HEAD_DIM ** -0.5
    s[CB_D_Q * 512:(CB_D_Q + 1) * 512] = DIFF_QK_DIM ** -0.5
    return jnp.asarray(s.reshape(IN_WIDTH // PROJ_TN, 1, PROJ_TN))


def kernel(x_prompt, x_sample, cache_na_k, cache_na_v, cache_diff_k, cache_diff_v, c, c_ctx, norm_g, w_mod, b_mod,
           w_in, na_rpb, fnet_w, diff_lq1, diff_lk1, diff_lq2, diff_lk2, diff_subln_g, w_branch, w_out, final_g):
    bp, n = x_prompt.shape[:2]
    bs, t = x_sample.shape[:2]
    assert n == 256 and t % (NA_KROWS * GRID_W) == 0 and 1 + bs <= 8

    w_in_b, w_mod_b = w_in.astype(BF16), w_mod.astype(BF16)
    wb_b, wo_b, wf_b = w_branch.astype(BF16), w_out.astype(BF16), fnet_w.astype(BF16)

    cvecs = jnp.zeros((8, D_MODEL), F32).at[0].set(c_ctx).at[1:1 + bs].set(c)
    mod = _modulation(cvecs, w_mod_b, b_mod).reshape(DEPTH, 8 * 3, 1, D_MODEL)

    colscale = _column_scales()
    rope_tabs = _rope_tables(t)
    fft_p, fft_s = _dft_constants(n), _dft_constants(t)
    na_kc = cache_na_k.transpose(0, 1, 3, 2, 4).reshape(bs, DEPTH, -1, NA_HEADS * NA_HEAD_DIM).astype(BF16)
    na_vc = cache_na_v.transpose(0, 1, 3, 2, 4).reshape(bs, DEPTH, -1, NA_HEADS * NA_HEAD_DIM).astype(BF16)
    rpb_flat = na_rpb.reshape(-1)
    fg = final_g.reshape(1, D_MODEL)

    tm_p, tm_s = 512, 1024
    tiles_per_seq = t // tm_s
    row_p = lambda i: 0
    row_s = lambda i: 1 + i // tiles_per_seq

    xp = x_prompt.reshape(bp * n, D_MODEL)
    xs = x_sample.reshape(bs * t, D_MODEL)
    kv = [[], [], [], []]
    for l in range(DEPTH):
        lam_init = 0.8 - 0.6 * math.exp(-0.3 * l)
        lam_params = [a[l].reshape(1, DIFF_QK_DIM) for a in (diff_lq1, diff_lk1, diff_lq2, diff_lk2)]
        sg = diff_subln_g[l].reshape(1, DIFF_V_DIM)
        ng = norm_g[l].reshape(1, D_MODEL)
        final = l == DEPTH - 1

        p, nak, nav, dk, dv = _projection(xp, mod[l], ng, w_in_b[l], colscale, tm=tm_p, mod_row=row_p, kv_out=True)
        for lst, a in zip(kv, (nak, nav, dk, dv)):
            lst.append(a)
        o_na, o_d = _prompt_attention(p, lam_params, sg, lam_init, batch=bp, seq=n)
        o_f = _fnet_small(p, fft_p["dft"], fft_p["cs"], wf_b[l], batch=bp, seq=n)
        xp = _merge(xp, o_na, o_f, o_d, p, mod[l], wb_b[l], wo_b[l], fg, tm=tm_p, mod_row=row_p, final=final)

        (p,) = _projection(xs, mod[l], ng, w_in_b[l], colscale, tm=tm_s, mod_row=row_s, rope_tabs=rope_tabs, seq=t)
        o_na = _neighbourhood_attention(p, na_kc, na_vc, rpb_flat, l, batch=bs, seq=t)
        o_d = _diff_attention(p, cache_diff_k, cache_diff_v, lam_params, sg, lam_init, l, batch=bs, seq=t)
        o_f = _fnet_large(p, fft_s["twc"], fft_s["tws"], fft_s["d3"], fft_s["cs"], wf_b[l], batch=bs, seq=t)
        xs = _merge(xs, o_na, o_f, o_d, p, mod[l], wb_b[l], wo_b[l], fg, tm=tm_s, mod_row=row_s, final=final)

    y_prompt = xp.reshape(bp, n, D_MODEL)
    y_sample = xs.reshape(bs, t, D_MODEL)
    new_na_k, new_na_v, new_diff_k, new_diff_v = (jnp.stack(a, axis=1) for a in kv)
    return (y_prompt, y_sample, new_na_k, new_na_v, new_diff_k, new_diff_v)
```

```python
import functools
import math

import jax
import jax.numpy as jnp
import numpy as np
from jax import lax
from jax.experimental import pallas as pl
from jax.experimental.pallas import tpu as pltpu

F32 = jnp.float32
BF16 = jnp.bfloat16

D_MODEL = 1024
DEPTH = 2
GRID_W = 64
NA_HEADS = 8
NA_HEAD_DIM = 64
NA_WIN_ROWS = 8
NA_WIN_COLS = 16
FNET_GROUPS = 4
FNET_GROUP_DIM = 128
DIFF_HEADS = 4
DIFF_QK_DIM = 64
DIFF_V_DIM = 128
BRANCH_WIDTH = 512
IN_WIDTH = 8192
ROPE_BASE = 10000.0
EPS = 1e-6
NEG_INF = -1e30

CB_NA_Q, CB_NA_K, CB_NA_V, CB_NA_Z, CB_F_U, CB_F_Z, CB_D_Q, CB_D_K, CB_D_V, CB_D_Z, CB_G = range(11)

LANES = 128
VMEM_LIMIT = 56 * 1024 * 1024

PROJ_TN = 1024
NA_QROWS = 8
NA_KROWS = 16
DIFF_TQ = 256
FFT_RADIX = 8


def _cparams(sem):
    return pltpu.CompilerParams(dimension_semantics=sem, vmem_limit_bytes=VMEM_LIMIT)


def _dot(a, b):
    return jnp.dot(a, b, preferred_element_type=F32)


def _dot_nt(a, b):
    return lax.dot_general(a, b, (((1,), (1,)), ((), ())), preferred_element_type=F32)


def _half_masks(dtype):
    lane = lax.broadcasted_iota(jnp.int32, (1, LANES), 1)
    lo = (lane < LANES // 2).astype(dtype)
    return lo, (1 - lo).astype(dtype)


def _mod_kernel(c_ref, w_ref, b_ref, o_ref):
    c = c_ref[...]
    s = c * jax.nn.sigmoid(c)
    o_ref[...] = _dot(s.astype(BF16), w_ref[...]) + b_ref[...]


def _modulation(cvecs, w_mod, b_mod):
    nt = 3 * D_MODEL // PROJ_TN
    return pl.pallas_call(
        _mod_kernel,
        grid=(DEPTH, nt),
        in_specs=[
            pl.BlockSpec((8, D_MODEL), lambda l, n: (0, 0)),
            pl.BlockSpec((None, D_MODEL, PROJ_TN), lambda l, n: (l, 0, n)),
            pl.BlockSpec((None, 1, PROJ_TN), lambda l, n: (l, 0, n)),
        ],
        out_specs=pl.BlockSpec((None, 8, PROJ_TN), lambda l, n: (l, 0, n)),
        out_shape=jax.ShapeDtypeStruct((DEPTH, 8, 3 * D_MODEL), F32),
        compiler_params=_cparams(("arbitrary", "arbitrary")),
        name="adaln_mod",
    )(cvecs, w_mod, b_mod.reshape(DEPTH, 1, 3 * D_MODEL))


def _rope_swap(x):
    w = x.shape[-1]
    lane = lax.broadcasted_iota(jnp.int32, (1, w), 1)
    first = (lane % 32) < 16
    return jnp.where(first, pltpu.roll(x, w - 16, 1), pltpu.roll(x, 16, 1))


def _proj_kernel(*refs, rope, kv_out, n_alias, tm):
    it = iter(refs)
    x_ref, shift_ref, scale_ref, ng_ref, w_ref, cs_ref = (next(it) for _ in range(6))
    cos_ref = sin_ref = None
    if rope:
        cos_ref, sin_ref = next(it), next(it)
    for _ in range(n_alias):
        next(it)
    p_ref = next(it)
    if kv_out:
        nak_ref, nav_ref, dk_ref, dv_ref = (next(it) for _ in range(4))
    h_scr = next(it)
    j = pl.program_id(1)

    @pl.when(j == 0)
    def _():
        x = x_ref[...]
        y = x * lax.rsqrt(jnp.mean(x * x, axis=-1, keepdims=True) + EPS) * ng_ref[...]
        h_scr[...] = (y * (1.0 + scale_ref[...]) + shift_ref[...]).astype(BF16)

    acc = _dot(h_scr[...], w_ref[...])

    if kv_out:
        nb = tm // 256

        def heads_out(ref, cols, width):
            for bb in range(nb):
                for h in range(512 // width):
                    ref[bb, h] = acc[bb * 256:(bb + 1) * 256, cols + h * width:cols + (h + 1) * width]

        @pl.when(j == 0)
        def _():
            heads_out(nak_ref, 512, NA_HEAD_DIM)

        @pl.when(j == 1)
        def _():
            heads_out(nav_ref, 0, NA_HEAD_DIM)

        @pl.when(j == 3)
        def _():
            heads_out(dk_ref, 512, 2 * DIFF_QK_DIM)

        @pl.when(j == 4)
        def _():
            heads_out(dv_ref, 0, DIFF_V_DIM)

    if rope:
        @pl.when(j == 3)
        def _():
            c = jnp.tile(cos_ref[...], (1, PROJ_TN // LANES))
            s = jnp.tile(sin_ref[...], (1, PROJ_TN // LANES))
            r = acc * c + _rope_swap(acc) * s
            p_ref[...] = (r * cs_ref[...]).astype(BF16)

        @pl.when(j != 3)
        def _():
            p_ref[...] = (acc * cs_ref[...]).astype(BF16)
    else:
        p_ref[...] = (acc * cs_ref[...]).astype(BF16)


def _projection(x2d, mod, ng, w_in, colscale, layer, *, tm, mod_row, rope_tabs=None, kv_out=False, kv_prev=None,
                seq=None):
    m = x2d.shape[0]
    nm, nn = m // tm, IN_WIDTH // PROJ_TN
    rope = rope_tabs is not None

    def mod_spec(part):
        return pl.BlockSpec((None, None, 1, D_MODEL), lambda i, j: (layer, mod_row(i) * 3 + part, 0, 0))

    in_specs = [
        pl.BlockSpec((tm, D_MODEL), lambda i, j: (i, 0)),
        mod_spec(0), mod_spec(1),
        pl.BlockSpec((None, 1, D_MODEL), lambda i, j: (layer, 0, 0)),
        pl.BlockSpec((None, D_MODEL, PROJ_TN), lambda i, j: (layer, 0, j)),
        pl.BlockSpec((None, 1, PROJ_TN), lambda i, j: (j, 0, 0)),
    ]
    args = [x2d, mod, mod, ng, w_in, colscale]
    if rope:
        tiles_per_seq = seq // tm
        in_specs += [pl.BlockSpec((tm, LANES), lambda i, j: (i % tiles_per_seq, 0))] * 2
        args += list(rope_tabs)
    out_specs = [pl.BlockSpec((tm, PROJ_TN), lambda i, j: (i, j))]
    out_shape = [jax.ShapeDtypeStruct((m, IN_WIDTH), BF16)]
    aliases = {}
    if kv_out:
        nb, b = tm // 256, m // 256
        for heads, dh in ((NA_HEADS, NA_HEAD_DIM), (NA_HEADS, NA_HEAD_DIM),
                          (DIFF_HEADS, 2 * DIFF_QK_DIM), (DIFF_HEADS, DIFF_V_DIM)):
            out_specs.append(pl.BlockSpec((nb, None, heads, 256, dh), lambda i, j: (i, layer, 0, 0, 0)))
            out_shape.append(jax.ShapeDtypeStruct((b, DEPTH, heads, 256, dh), F32))
        if kv_prev is not None:
            for k, a in enumerate(kv_prev):
                aliases[len(args)] = 1 + k
                in_specs.append(pl.BlockSpec(memory_space=pl.ANY))
                args.append(a)
    return pl.pallas_call(
        functools.partial(_proj_kernel, rope=rope, kv_out=kv_out, n_alias=len(aliases), tm=tm),
        grid=(nm, nn),
        in_specs=in_specs,
        out_specs=out_specs,
        out_shape=out_shape,
        input_output_aliases=aliases,
        scratch_shapes=[pltpu.VMEM((tm, D_MODEL), BF16)],
        compiler_params=_cparams(("arbitrary", "arbitrary")),
        name="in_proj_kv" if kv_out else "in_proj_rope",
    )(*args)


def _diff_lambda(lq1_ref, lk1_ref, lq2_ref, lk2_ref, lam_init):
    a = jnp.sum(lq1_ref[...] * lk1_ref[...], axis=-1, keepdims=True)
    b = jnp.sum(lq2_ref[...] * lk2_ref[...], axis=-1, keepdims=True)
    return jnp.exp(a) - jnp.exp(b) + lam_init


def _subln(o, g, lam_init):
    y = o * lax.rsqrt(jnp.mean(o * o, axis=-1, keepdims=True) + EPS)
    return (y * g) * (1.0 - lam_init)


def _prompt_attn_kernel(qa_ref, ka_ref, va_ref, qd_ref, kd_ref, vd_ref,
                        lq1_ref, lk1_ref, lq2_ref, lk2_ref, sg_ref, ona_ref, od_ref, *, lam_init):
    n = qa_ref.shape[0]
    mlo, mhi = _half_masks(BF16)
    for c in range(NA_HEADS // 2):
        sl = slice(c * LANES, (c + 1) * LANES)
        q, k, v = qa_ref[:, sl], ka_ref[:, sl], va_ref[:, sl]
        s = _dot_nt(jnp.concatenate([q * mlo, q * mhi], axis=0), k)
        e = jnp.exp(s - jnp.max(s, axis=-1, keepdims=True))
        p = (e / jnp.sum(e, axis=-1, keepdims=True)).astype(BF16)
        pcat = jnp.concatenate([p[:n], p[n:]], axis=1)
        vbd = jnp.concatenate([v * mlo, v * mhi], axis=0)
        ona_ref[:, sl] = _dot(pcat, vbd).astype(BF16)
    lam = _diff_lambda(lq1_ref, lk1_ref, lq2_ref, lk2_ref, lam_init)
    for h in range(DIFF_HEADS):
        sl = slice(h * LANES, (h + 1) * LANES)
        q, k, v = qd_ref[:, sl], kd_ref[:, sl], vd_ref[:, sl]
        s = _dot_nt(jnp.concatenate([q * mlo, q * mhi], axis=0), k)
        e = jnp.exp(s - jnp.max(s, axis=-1, keepdims=True))
        p = e / jnp.sum(e, axis=-1, keepdims=True)
        a = (p[:n] - lam * p[n:]).astype(BF16)
        od_ref[:, sl] = _subln(_dot(a, v), sg_ref[...], lam_init).astype(BF16)


def _prompt_attention(p, lam_params, sg, lam_init, layer, *, batch, seq):
    def cb(k):
        return pl.BlockSpec((seq, BRANCH_WIDTH), lambda b: (b, k))
    small = pl.BlockSpec((None, 1, DIFF_QK_DIM), lambda b: (layer, 0, 0))
    out = pl.BlockSpec((seq, BRANCH_WIDTH), lambda b: (b, 0))
    return pl.pallas_call(
        functools.partial(_prompt_attn_kernel, lam_init=lam_init),
        grid=(batch,),
        in_specs=[cb(CB_NA_Q), cb(CB_NA_K), cb(CB_NA_V), cb(CB_D_Q), cb(CB_D_K), cb(CB_D_V),
                  small, small, small, small, pl.BlockSpec((None, 1, DIFF_V_DIM), lambda b: (layer, 0, 0))],
        out_specs=[out, out],
        out_shape=[jax.ShapeDtypeStruct((batch * seq, BRANCH_WIDTH), BF16)] * 2,
        compiler_params=_cparams(("arbitrary",)),
        name="prompt_attn",
    )(p, p, p, p, p, p, *lam_params, sg)


def _na_variant_geometry(variant, rows):
    jv = (0, 1, rows // NA_QROWS - 1)[variant]
    kstart = min(max(NA_QROWS * jv - NA_WIN_ROWS // 2, 0), rows - NA_KROWS)
    return jv, kstart


def _na_build_bias(rpb_ref, bias_scr, layer, pair, rows):
    cq = lax.broadcasted_iota(jnp.int32, (GRID_W, GRID_W), 0)
    ck = lax.broadcasted_iota(jnp.int32, (GRID_W, GRID_W), 1)
    col_start = jnp.clip(cq - NA_WIN_COLS // 2, 0, GRID_W - NA_WIN_COLS)
    col_ok = (ck >= col_start) & (ck < col_start + NA_WIN_COLS)
    dc = jnp.clip(ck - cq + NA_WIN_COLS - 1, 0, 2 * NA_WIN_COLS - 2)
    n_dr, n_dc = 2 * NA_WIN_ROWS - 1, 2 * NA_WIN_COLS - 1
    neg = jnp.full((GRID_W, GRID_W), NEG_INF, F32)
    for hh in range(2):
        base = ((layer * NA_HEADS + 2 * pair + hh) * n_dr) * n_dc
        tabs = []
        for a in range(n_dr):
            t = jnp.zeros((GRID_W, GRID_W), F32)
            for b in range(n_dc):
                t = jnp.where(dc == b, rpb_ref[base + a * n_dc + b], t)
            tabs.append(jnp.where(col_ok, t, NEG_INF))
        for variant in range(3):
            jv, kstart = _na_variant_geometry(variant, rows)
            for i in range(NA_QROWS):
                r = NA_QROWS * jv + i
                rs = min(max(r - NA_WIN_ROWS // 2, 0), rows - NA_WIN_ROWS)
                blocks = []
                for u in range(NA_KROWS):
                    kr = kstart + u
                    blocks.append(tabs[kr - r + NA_WIN_ROWS - 1] if rs <= kr < rs + NA_WIN_ROWS else neg)
                bias_scr[hh, variant, i * GRID_W:(i + 1) * GRID_W, :] = jnp.concatenate(blocks, axis=1)


def _na_kernel(rpb_ref, q_ref, k_ref, v_ref, kc_ref, vc_ref, o_ref, bias_scr, *, layer, rows):
    pair, b, j = pl.program_id(0), pl.program_id(1), pl.program_id(2)
    nj = rows // NA_QROWS
    nq, nk = NA_QROWS * GRID_W, NA_KROWS * GRID_W

    @pl.when((b == 0) & (j == 0))
    def _():
        _na_build_bias(rpb_ref, bias_scr, layer, pair, rows)

    variant = jnp.where(j == 0, 0, jnp.where(j == nj - 1, 2, 1))
    kstart = jnp.clip(NA_QROWS * j - NA_WIN_ROWS // 2, 0, rows - NA_KROWS)
    k0 = pl.multiple_of(kstart * GRID_W, 256)
    mlo, mhi = _half_masks(BF16)
    q = q_ref[...]
    qq = jnp.concatenate([q * mlo, q * mhi], axis=0)
    kl, vl = k_ref[pl.ds(k0, nk), :], v_ref[pl.ds(k0, nk), :]
    kc, vc = kc_ref[...], vc_ref[...]
    bias = jnp.concatenate([bias_scr[0, variant], bias_scr[1, variant]], axis=0)
    s_loc = _dot_nt(qq, kl) + bias
    s_ctx = _dot_nt(qq, kc)
    m = jnp.maximum(jnp.max(s_loc, axis=-1, keepdims=True), jnp.max(s_ctx, axis=-1, keepdims=True))
    e_loc, e_ctx = jnp.exp(s_loc - m), jnp.exp(s_ctx - m)
    l = jnp.sum(e_loc, axis=-1, keepdims=True) + jnp.sum(e_ctx, axis=-1, keepdims=True)
    e_loc, e_ctx = e_loc.astype(BF16), e_ctx.astype(BF16)
    pcat = jnp.concatenate([e_loc[:nq], e_ctx[:nq], e_loc[nq:], e_ctx[nq:]], axis=1)
    vbd = jnp.concatenate([vl * mlo, vc * mlo, vl * mhi, vc * mhi], axis=0)
    o = _dot(pcat, vbd)
    lane = lax.broadcasted_iota(jnp.int32, (1, LANES), 1)
    inv = 1.0 / l
    o_ref[...] = (o * jnp.where(lane < LANES // 2, inv[:nq], inv[nq:])).astype(BF16)


def _neighbourhood_attention(p, kc, vc, rpb_flat, layer, *, batch, seq):
    rows = seq // GRID_W
    nj = rows // NA_QROWS
    nq = NA_QROWS * GRID_W
    n_ctx = kc.shape[2]
    cpb = BRANCH_WIDTH // LANES
    return pl.pallas_call(
        functools.partial(_na_kernel, layer=layer, rows=rows),
        grid=(NA_HEADS // 2, batch, nj),
        in_specs=[
            pl.BlockSpec(memory_space=pltpu.SMEM),
            pl.BlockSpec((nq, LANES), lambda c, b, j: (b * nj + j, CB_NA_Q * cpb + c)),
            pl.BlockSpec((seq, LANES), lambda c, b, j: (b, CB_NA_K * cpb + c)),
            pl.BlockSpec((seq, LANES), lambda c, b, j: (b, CB_NA_V * cpb + c)),
            pl.BlockSpec((None, None, n_ctx, LANES), lambda c, b, j: (b, layer, 0, c)),
            pl.BlockSpec((None, None, n_ctx, LANES), lambda c, b, j: (b, layer, 0, c)),
        ],
        out_specs=pl.BlockSpec((nq, LANES), lambda c, b, j: (b * nj + j, c)),
        out_shape=jax.ShapeDtypeStruct((batch * seq, BRANCH_WIDTH), BF16),
        scratch_shapes=[pltpu.VMEM((2, 3, nq, NA_KROWS * GRID_W), F32)],
        compiler_params=_cparams(("arbitrary", "arbitrary", "arbitrary")),
        name="na_attn",
    )(rpb_flat, p, p, p, kc, vc)


def _diff_kernel(q_ref, k_ref, v_ref, kc_ref, vc_ref, lq1_ref, lk1_ref, lq2_ref, lk2_ref, sg_ref, o_ref,
                 kall_scr, vbd_scr, s_ev, s_od, m_ev, m_od, e_ev, e_od, inv_ev, inv_od,
                 *, lam_init, seq, steps_per_head, n_steps):
    g = pl.program_id(0)
    tq = DIFF_TQ
    n_all = kall_scr.shape[0]
    mlo, mhi = _half_masks(BF16)
    head = jnp.minimum(g, n_steps - 1) // steps_per_head
    prev_head = jnp.maximum(g - 1, 0) // steps_per_head

    @pl.when(g == 0)
    def _():
        for r in (s_od, m_od, e_ev, e_od, inv_ev, inv_od):
            r[...] = jnp.zeros(r.shape, r.dtype)

    @pl.when((g % steps_per_head == 0) & (g < n_steps))
    def _():
        kall_scr[0:seq, :] = k_ref[...]
        kall_scr[seq:n_all, :] = kc_ref[...].astype(BF16)
        v_all = jnp.concatenate([v_ref[...], vc_ref[...].astype(BF16)], axis=0)
        z = jnp.zeros_like(v_all)
        vbd_scr[head % 2, 0:n_all, :] = jnp.concatenate([v_all, z], axis=1)
        vbd_scr[head % 2, n_all:2 * n_all, :] = jnp.concatenate([z, v_all], axis=1)

    lam = _diff_lambda(lq1_ref, lk1_ref, lq2_ref, lk2_ref, lam_init)

    def stage_a(rows, s_scr, m_scr):
        q = q_ref[rows, :]
        s = _dot_nt(jnp.concatenate([q * mlo, q * mhi], axis=0), kall_scr[...])
        s_scr[...] = s
        m_scr[...] = jnp.max(s, axis=-1, keepdims=True)

    def stage_b(s_scr, m_scr, e_scr, inv_scr):
        e = jnp.exp(s_scr[...] - m_scr[...])
        inv_scr[...] = 1.0 / jnp.sum(e, axis=-1, keepdims=True)
        e = e.astype(BF16)
        e_scr[:, 0:n_all] = e[:tq]
        e_scr[:, n_all:2 * n_all] = e[tq:]

    def stage_c(rows, e_scr, inv_scr):
        o12 = _dot(e_scr[...], vbd_scr[prev_head % 2])
        inv = inv_scr[...]
        o = o12[:, :DIFF_V_DIM] * inv[:tq] - lam * (o12[:, DIFF_V_DIM:] * inv[tq:])
        o_ref[rows, :] = _subln(o, sg_ref[...], lam_init).astype(BF16)

    first, second = slice(0, tq), slice(tq, 2 * tq)
    stage_a(first, s_ev, m_ev)
    stage_b(s_od, m_od, e_od, inv_od)
    stage_c(first, e_ev, inv_ev)
    stage_a(second, s_od, m_od)
    stage_b(s_ev, m_ev, e_ev, inv_ev)
    stage_c(second, e_od, inv_od)


def _diff_attention(p, kc, vc, lam_params, sg, lam_init, layer, *, batch, seq):
    nq = seq // DIFF_TQ
    sph = nq // 2
    n_steps = batch * DIFF_HEADS * sph
    n_ctx = kc.shape[3]
    n_all = seq + n_ctx
    cpb = BRANCH_WIDTH // LANES
    tq2 = 2 * DIFF_TQ

    def a_slot(g):
        hd = jnp.minimum(g, n_steps - 1) // sph
        return hd // DIFF_HEADS, hd % DIFF_HEADS, jnp.minimum(g, n_steps - 1) % sph

    def c_slot(g):
        gg = jnp.maximum(g - 1, 0)
        hd = gg // sph
        return hd // DIFF_HEADS, hd % DIFF_HEADS, gg % sph

    def q_map(g):
        b, h, i = a_slot(g)
        return b * sph + i, CB_D_Q * cpb + h

    def o_map(g):
        b, h, i = c_slot(g)
        return b * sph + i, h

    small = pl.BlockSpec((None, 1, DIFF_QK_DIM), lambda g: (layer, 0, 0))
    ctx = pl.BlockSpec((None, None, None, n_ctx, LANES), lambda g: (a_slot(g)[0], layer, a_slot(g)[1], 0, 0))
    return pl.pallas_call(
        functools.partial(_diff_kernel, lam_init=lam_init, seq=seq, steps_per_head=sph, n_steps=n_steps),
        grid=(n_steps + 1,),
        in_specs=[
            pl.BlockSpec((tq2, LANES), q_map),
            pl.BlockSpec((seq, LANES), lambda g: (a_slot(g)[0], CB_D_K * cpb + a_slot(g)[1])),
            pl.BlockSpec((seq, LANES), lambda g: (a_slot(g)[0], CB_D_V * cpb + a_slot(g)[1])),
            ctx, ctx, small, small, small, small,
            pl.BlockSpec((None, 1, DIFF_V_DIM), lambda g: (layer, 0, 0)),
        ],
        out_specs=pl.BlockSpec((tq2, LANES), o_map),
        out_shape=jax.ShapeDtypeStruct((batch * seq, BRANCH_WIDTH), BF16),
        scratch_shapes=[
            pltpu.VMEM((n_all, LANES), BF16), pltpu.VMEM((2, 2 * n_all, 2 * LANES), BF16),
            pltpu.VMEM((tq2, n_all), F32), pltpu.VMEM((tq2, n_all), F32),
            pltpu.VMEM((tq2, 1), F32), pltpu.VMEM((tq2, 1), F32),
            pltpu.VMEM((DIFF_TQ, 2 * n_all), BF16), pltpu.VMEM((DIFF_TQ, 2 * n_all), BF16),
            pltpu.VMEM((tq2, 1), F32), pltpu.VMEM((tq2, 1), F32),
        ],
        compiler_params=_cparams(("arbitrary",)),
        name="diff_attn",
    )(p, p, p, kc, vc, *lam_params, sg)


def _fnet_tail(x3, cs_ref, wf_ref, n):
    f = _dot(x3[:n].astype(BF16), cs_ref[0:BRANCH_WIDTH, :]) + _dot(x3[n:].astype(BF16), cs_ref[BRANCH_WIDTH:, :])
    return _dot(f.astype(BF16), wf_ref[...])


def _fnet_small_kernel(u_ref, dft_ref, cs_ref, wf_ref, o_ref):
    n = u_ref.shape[0]
    x3 = _dot(dft_ref[...], u_ref[...])
    o_ref[...] = _fnet_tail(x3, cs_ref, wf_ref, n).astype(BF16)


def _fnet_small(p, dft, cs, wf, layer, *, batch, seq):
    return pl.pallas_call(
        _fnet_small_kernel,
        grid=(batch,),
        in_specs=[
            pl.BlockSpec((seq, BRANCH_WIDTH), lambda b: (b, CB_F_U)),
            pl.BlockSpec(dft.shape, lambda b: (0, 0)),
            pl.BlockSpec(cs.shape, lambda b: (0, 0)),
            pl.BlockSpec((None,) + wf.shape[1:], lambda b: (layer, 0, 0)),
        ],
        out_specs=pl.BlockSpec((seq, BRANCH_WIDTH), lambda b: (b, 0)),
        out_shape=jax.ShapeDtypeStruct((batch * seq, BRANCH_WIDTH), BF16),
        compiler_params=_cparams(("arbitrary",)),
        name="fnet_prompt",
    )(p, dft, cs, wf)


FFT_CHUNK = 16


def _fft_stage1(u_ref, twc_ref, tws_ref, s_scr, n2):
    h = math.sqrt(0.5)
    rep = BRANCH_WIDTH // LANES

    def body(ci, carry):
        r0 = pl.multiple_of(ci * FFT_CHUNK, FFT_CHUNK)
        a = [u_ref[pl.ds(r * n2 + r0, FFT_CHUNK), :].astype(F32) for r in range(FFT_RADIX)]
        s04, d04, s26, d26 = a[0] + a[4], a[0] - a[4], a[2] + a[6], a[2] - a[6]
        s15, d15, s37, d37 = a[1] + a[5], a[1] - a[5], a[3] + a[7], a[3] - a[7]
        ee, oo = s04 + s26, s15 + s37
        dm, dp = h * (d15 - d37), h * (d15 + d37)
        zero = jnp.zeros_like(ee)
        re = [ee + oo, d04 + dm, s04 - s26, d04 - dm, ee - oo]
        im = [zero, dp + d26, s15 - s37, dp - d26, zero]
        for k1 in range(FFT_RADIX):
            kk = k1 if k1 <= FFT_RADIX // 2 else FFT_RADIX - k1
            ar, ai = re[kk], (im[kk] if k1 <= FFT_RADIX // 2 else -im[kk])
            ct = jnp.tile(twc_ref[k1, pl.ds(r0, FFT_CHUNK), :], (1, rep))
            st = jnp.tile(tws_ref[k1, pl.ds(r0, FFT_CHUNK), :], (1, rep))
            s_scr[k1, pl.ds(r0, FFT_CHUNK), :] = (ar * ct - ai * st).astype(BF16)
            s_scr[k1, pl.ds(n2 + r0, FFT_CHUNK), :] = (ar * st + ai * ct).astype(BF16)
        return carry

    lax.fori_loop(0, n2 // FFT_CHUNK, body, 0)


def _fnet_large_kernel(u_ref, twc_ref, tws_ref, d3_ref, cs_ref, wf_ref, o_ref, s_scr):
    k1 = pl.program_id(1)
    n2 = s_scr.shape[1] // 2

    @pl.when(k1 == 0)
    def _():
        _fft_stage1(u_ref, twc_ref, tws_ref, s_scr, n2)

    x3 = _dot(d3_ref[...], s_scr[k1])
    o_ref[...] = _fnet_tail(x3, cs_ref, wf_ref, n2).astype(BF16)


def _fnet_large(p, twc, tws, d3, cs, wf, layer, *, batch, seq):
    n2 = seq // FFT_RADIX
    out = pl.pallas_call(
        _fnet_large_kernel,
        grid=(batch, FFT_RADIX),
        in_specs=[
            pl.BlockSpec((seq, BRANCH_WIDTH), lambda b, k: (b, CB_F_U)),
            pl.BlockSpec(twc.shape, lambda b, k: (0, 0, 0)),
            pl.BlockSpec(tws.shape, lambda b, k: (0, 0, 0)),
            pl.BlockSpec(d3.shape, lambda b, k: (0, 0)),
            pl.BlockSpec(cs.shape, lambda b, k: (0, 0)),
            pl.BlockSpec((None,) + wf.shape[1:], lambda b, k: (layer, 0, 0)),
        ],
        out_specs=pl.BlockSpec((None, n2, BRANCH_WIDTH), lambda b, k: (b, 0, k)),
        out_shape=jax.ShapeDtypeStruct((batch, n2, FFT_RADIX * BRANCH_WIDTH), BF16),
        scratch_shapes=[pltpu.VMEM((FFT_RADIX, 2 * n2, BRANCH_WIDTH), BF16)],
        compiler_params=_cparams(("arbitrary", "arbitrary")),
        name="fnet_sample",
    )(p, twc, tws, d3, cs, wf)
    return out.reshape(batch * seq, BRANCH_WIDTH)


def _dft_constants(seq):
    gd = FNET_GROUP_DIM
    norm = 1.0 / math.sqrt(seq * gd)
    ang = 2.0 * np.pi * (np.outer(np.arange(gd), np.arange(gd)) % gd) / gd
    eye = np.eye(FNET_GROUPS)
    cs = np.concatenate([np.kron(eye, np.cos(ang)), -np.kron(eye, np.sin(ang))], axis=0) * norm
    if seq % (FFT_RADIX * FFT_CHUNK) != 0 or seq < 2048:
        ang_t = 2.0 * np.pi * (np.outer(np.arange(seq), np.arange(seq)) % seq) / seq
        dft = np.concatenate([np.cos(ang_t), np.sin(ang_t)], axis=0)
        return dict(dft=jnp.asarray(dft, F32).astype(BF16), cs=jnp.asarray(cs, F32).astype(BF16))
    n2 = seq // FFT_RADIX
    ang2 = 2.0 * np.pi * (np.outer(np.arange(n2), np.arange(n2)) % n2) / n2
    c2, s2 = np.cos(ang2), np.sin(ang2)
    d3 = np.block([[c2, -s2], [s2, c2]])
    angw = 2.0 * np.pi * np.outer(np.arange(FFT_RADIX), np.arange(n2)) / seq
    twc = np.repeat(np.cos(angw)[:, :, None], LANES, axis=2)
    tws = np.repeat(np.sin(angw)[:, :, None], LANES, axis=2)
    return dict(d3=jnp.asarray(d3, F32).astype(BF16), cs=jnp.asarray(cs, F32).astype(BF16),
                twc=jnp.asarray(twc, F32), tws=jnp.asarray(tws, F32))


def _merge_kernel(x_ref, ona_ref, of_ref, od_ref, zna_ref, zf_ref, zd_ref, g0_ref, g1_ref, g2_ref,
                  gate_ref, wb_ref, wo_ref, fg_ref, o_ref, *, final):
    y = None
    for b, (o_b, z_b, g_b) in enumerate(((ona_ref, zna_ref, g0_ref), (of_ref, zf_ref, g1_ref), (od_ref, zd_ref, g2_ref))):
        z = z_b[...].astype(F32)
        u = (o_b[...].astype(F32) * (z * jax.nn.sigmoid(z))).astype(BF16)
        t = jax.nn.sigmoid(g_b[...].astype(F32)) * _dot(u, wb_ref[b])
        y = t if y is None else y + t
    xn = x_ref[...] + gate_ref[...] * _dot(y.astype(BF16), wo_ref[...])
    if final:
        xn = xn * lax.rsqrt(jnp.mean(xn * xn, axis=-1, keepdims=True) + EPS) * fg_ref[...]
    o_ref[...] = xn


def _merge(x2d, o_na, o_f, o_d, p, mod, wb, wo, fg, layer, *, tm, mod_row, final):
    m = x2d.shape[0]

    def cb(k):
        return pl.BlockSpec((tm, BRANCH_WIDTH), lambda i: (i, k))

    def gb(k):
        return pl.BlockSpec((tm, D_MODEL), lambda i: (i, CB_G // 2 + k))

    br = pl.BlockSpec((tm, BRANCH_WIDTH), lambda i: (i, 0))
    return pl.pallas_call(
        functools.partial(_merge_kernel, final=final),
        grid=(m // tm,),
        in_specs=[
            pl.BlockSpec((tm, D_MODEL), lambda i: (i, 0)), br, br, br,
            cb(CB_NA_Z), cb(CB_F_Z), cb(CB_D_Z), gb(0), gb(1), gb(2),
            pl.BlockSpec((None, None, 1, D_MODEL), lambda i: (layer, mod_row(i) * 3 + 2, 0, 0)),
            pl.BlockSpec((None,) + wb.shape[1:], lambda i: (layer, 0, 0, 0)),
            pl.BlockSpec((None,) + wo.shape[1:], lambda i: (layer, 0, 0)),
            pl.BlockSpec((1, D_MODEL), lambda i: (0, 0)),
        ],
        out_specs=pl.BlockSpec((tm, D_MODEL), lambda i: (i, 0)),
        out_shape=jax.ShapeDtypeStruct((m, D_MODEL), F32),
        compiler_params=_cparams(("arbitrary",)),
        name="merge_final" if final else "merge",
    )(x2d, o_na, o_f, o_d, p, p, p, p, p, p, mod, wb, wo, fg)


def _rope_tables(seq):
    t = np.arange(seq)
    row = (t // GRID_W).astype(np.float64)
    col = (t % GRID_W).astype(np.float64)
    half = DIFF_QK_DIM // 2
    inv = ROPE_BASE ** (-np.arange(0, half, 2, dtype=np.float64) / half)
    ar, ac = row[:, None] * inv, col[:, None] * inv
    cr, sr, cc, sc = np.cos(ar), np.sin(ar), np.cos(ac), np.sin(ac)
    cos = np.concatenate([cr, cr, cc, cc] * 2, axis=-1)
    sin = np.concatenate([-sr, sr, -sc, sc] * 2, axis=-1)
    return jnp.asarray(cos, F32), jnp.asarray(sin, F32)


def _column_scales():
    s = np.ones((IN_WIDTH,), np.float32)
    s[CB_NA_Q * 512:(CB_NA_Q + 1) * 512] = NA_HEAD_DIM ** -0.5
    s[CB_D_Q * 512:(CB_D_Q + 1) * 512] = DIFF_QK_DIM ** -0.5
    return jnp.asarray(s.reshape(IN_WIDTH // PROJ_TN, 1, PROJ_TN))


def kernel(x_prompt, x_sample, cache_na_k, cache_na_v, cache_diff_k, cache_diff_v, c, c_ctx, norm_g, w_mod, b_mod,
           w_in, na_rpb, fnet_w, diff_lq1, diff_lk1, diff_lq2, diff_lk2, diff_subln_g, w_branch, w_out, final_g):
    bp, n = x_prompt.shape[:2]
    bs, t = x_sample.shape[:2]
    assert n == 256 and t % (NA_KROWS * GRID_W) == 0 and 1 + bs <= 8

    w_in_b, w_mod_b = w_in.astype(BF16), w_mod.astype(BF16)
    wb_b, wo_b, wf_b = w_branch.astype(BF16), w_out.astype(BF16), fnet_w.astype(BF16)

    cvecs = jnp.concatenate([c_ctx[None], c, jnp.zeros((8 - 1 - bs, D_MODEL), F32)], axis=0)
    mod = _modulation(cvecs, w_mod_b, b_mod).reshape(DEPTH, 8 * 3, 1, D_MODEL)

    colscale = _column_scales()
    rope_tabs = _rope_tables(t)
    fft_p, fft_s = _dft_constants(n), _dft_constants(t)
    na_kc = cache_na_k.transpose(0, 1, 3, 2, 4).reshape(bs, DEPTH, -1, NA_HEADS * NA_HEAD_DIM).astype(BF16)
    na_vc = cache_na_v.transpose(0, 1, 3, 2, 4).reshape(bs, DEPTH, -1, NA_HEADS * NA_HEAD_DIM).astype(BF16)
    rpb_flat = na_rpb.reshape(-1)
    fg = final_g.reshape(1, D_MODEL)

    tm_p, tm_s = 512, 1024
    tiles_per_seq = t // tm_s
    row_p = lambda i: 0
    row_s = lambda i: 1 + i // tiles_per_seq

    xp = x_prompt.reshape(bp * n, D_MODEL)
    xs = x_sample.reshape(bs * t, D_MODEL)
    lam_params = [a.reshape(DEPTH, 1, DIFF_QK_DIM) for a in (diff_lq1, diff_lk1, diff_lq2, diff_lk2)]
    sg = diff_subln_g.reshape(DEPTH, 1, DIFF_V_DIM)
    ng = norm_g.reshape(DEPTH, 1, D_MODEL)
    kv = None
    for l in range(DEPTH):
        lam_init = 0.8 - 0.6 * math.exp(-0.3 * l)
        final = l == DEPTH - 1

        p, *kv = _projection(xp, mod, ng, w_in_b, colscale, l, tm=tm_p, mod_row=row_p, kv_out=True, kv_prev=kv)
        o_na, o_d = _prompt_attention(p, lam_params, sg, lam_init, l, batch=bp, seq=n)
        o_f = _fnet_small(p, fft_p["dft"], fft_p["cs"], wf_b, l, batch=bp, seq=n)
        xp = _merge(xp, o_na, o_f, o_d, p, mod, wb_b, wo_b, fg, l, tm=tm_p, mod_row=row_p, final=final)

        (p,) = _projection(xs, mod, ng, w_in_b, colscale, l, tm=tm_s, mod_row=row_s, rope_tabs=rope_tabs, seq=t)
        o_na = _neighbourhood_attention(p, na_kc, na_vc, rpb_flat, l, batch=bs, seq=t)
        o_d = _diff_attention(p, cache_diff_k, cache_diff_v, lam_params, sg, lam_init, l, batch=bs, seq=t)
        o_f = _fnet_large(p, fft_s["twc"], fft_s["tws"], fft_s["d3"], fft_s["cs"], wf_b, l, batch=bs, seq=t)
        xs = _merge(xs, o_na, o_f, o_d, p, mod, wb_b, wo_b, fg, l, tm=tm_s, mod_row=row_s, final=final)

    y_prompt = xp.reshape(bp, n, D_MODEL)
    y_sample = xs.reshape(bs, t, D_MODEL)
    new_na_k, new_na_v, new_diff_k, new_diff_v = kv
    return (y_prompt, y_sample, new_na_k, new_na_v, new_diff_k, new_diff_v)
```

```python
import functools
import math

import jax
import jax.numpy as jnp
import numpy as np
from jax import lax
from jax.experimental import pallas as pl
from jax.experimental.pallas import tpu as pltpu

F32 = jnp.float32
BF16 = jnp.bfloat16

D_MODEL = 1024
DEPTH = 2
GRID_W = 64
NA_HEADS = 8
NA_HEAD_DIM = 64
NA_WIN_ROWS = 8
NA_WIN_COLS = 16
FNET_GROUPS = 4
FNET_GROUP_DIM = 128
DIFF_HEADS = 4
DIFF_QK_DIM = 64
DIFF_V_DIM = 128
BRANCH_WIDTH = 512
IN_WIDTH = 8192
ROPE_BASE = 10000.0
EPS = 1e-6
NEG_INF = -1e30

CB_NA_Q, CB_NA_K, CB_NA_V, CB_NA_Z, CB_F_U, CB_F_Z, CB_D_Q, CB_D_K, CB_D_V, CB_D_Z, CB_G = range(11)

LANES = 128
VREGS = 64
VMEM_LIMIT = 56 * 1024 * 1024

PROJ_TN = 1024
NA_QROWS = 8
NA_KROWS = 16
DIFF_TQ = 256
SOFTMAX_MAX_ROWS = 8
SOFTMAX_EXP_ROWS = 16
FFT_RADIX = 8


def _cparams(sem):
    return pltpu.CompilerParams(dimension_semantics=sem, vmem_limit_bytes=VMEM_LIMIT)


def _dot(a, b):
    return jnp.dot(a, b, preferred_element_type=F32)


def _dot_nt(a, b):
    return lax.dot_general(a, b, (((1,), (1,)), ((), ())), preferred_element_type=F32)


def _half_masks(dtype):
    lane = lax.broadcasted_iota(jnp.int32, (1, LANES), 1)
    lo = (lane < LANES // 2).astype(dtype)
    return lo, (1 - lo).astype(dtype)


def _mod_kernel(c_ref, w_ref, b_ref, o_ref):
    c = c_ref[...]
    s = c * jax.nn.sigmoid(c)
    o_ref[...] = _dot(s.astype(BF16), w_ref[...]) + b_ref[...]


def _modulation(cvecs, w_mod, b_mod):
    nt = 3 * D_MODEL // PROJ_TN
    return pl.pallas_call(
        _mod_kernel,
        grid=(DEPTH, nt),
        in_specs=[
            pl.BlockSpec((8, D_MODEL), lambda l, n: (0, 0)),
            pl.BlockSpec((None, D_MODEL, PROJ_TN), lambda l, n: (l, 0, n)),
            pl.BlockSpec((None, 1, PROJ_TN), lambda l, n: (l, 0, n)),
        ],
        out_specs=pl.BlockSpec((None, 8, PROJ_TN), lambda l, n: (l, 0, n)),
        out_shape=jax.ShapeDtypeStruct((DEPTH, 8, 3 * D_MODEL), F32),
        compiler_params=_cparams(("arbitrary", "arbitrary")),
        name="adaln_mod",
    )(cvecs, w_mod, b_mod.reshape(DEPTH, 1, 3 * D_MODEL))


def _rope_swap(x):
    w = x.shape[-1]
    lane = lax.broadcasted_iota(jnp.int32, (1, w), 1)
    first = (lane % 32) < 16
    return jnp.where(first, pltpu.roll(x, w - 16, 1), pltpu.roll(x, 16, 1))


def _proj_kernel(*refs, rope, kv_out, n_alias, tm, layer, nn):
    it = iter(refs)
    x_ref, shift_ref, scale_ref, ng_ref, w_ref, cs_ref = (next(it) for _ in range(6))
    cos_ref = sin_ref = None
    if rope:
        cos_ref, sin_ref = next(it), next(it)
    for _ in range(n_alias):
        next(it)
    p_ref = next(it)
    if kv_out:
        nak_ref, nav_ref, dk_ref, dv_ref = (next(it) for _ in range(4))
    h_scr = next(it)
    j = pl.program_id(1)

    @pl.when(j == 0)
    def _():
        x = x_ref[...]
        y = x * lax.rsqrt(jnp.mean(x * x, axis=-1, keepdims=True) + EPS) * ng_ref[...]
        h_scr[...] = (y * (1.0 + scale_ref[...]) + shift_ref[...]).astype(BF16)

    acc = _dot(h_scr[...], w_ref[...])

    if kv_out:
        nb = tm // 256

        def heads_out(ref, cols, width):
            dst = ref if n_alias else ref.at[:, layer]
            for bb in range(nb):
                for h in range(512 // width):
                    dst[bb, h] = acc[bb * 256:(bb + 1) * 256, cols + h * width:cols + (h + 1) * width]

        if not n_alias:
            @pl.when(j == nn - 1)
            def _():
                for ref in (nak_ref, nav_ref, dk_ref, dv_ref):
                    for other in range(ref.shape[1]):
                        if other != layer:
                            ref[:, other] = jnp.zeros((ref.shape[0],) + ref.shape[2:], ref.dtype)

        @pl.when(j == 0)
        def _():
            heads_out(nak_ref, 512, NA_HEAD_DIM)

        @pl.when(j == 1)
        def _():
            heads_out(nav_ref, 0, NA_HEAD_DIM)

        @pl.when(j == 3)
        def _():
            heads_out(dk_ref, 512, 2 * DIFF_QK_DIM)

        @pl.when(j == 4)
        def _():
            heads_out(dv_ref, 0, DIFF_V_DIM)

    if rope:
        @pl.when(j == 3)
        def _():
            c = jnp.tile(cos_ref[...], (1, PROJ_TN // LANES))
            s = jnp.tile(sin_ref[...], (1, PROJ_TN // LANES))
            r = acc * c + _rope_swap(acc) * s
            p_ref[...] = (r * cs_ref[...]).astype(BF16)

        @pl.when(j != 3)
        def _():
            p_ref[...] = (acc * cs_ref[...]).astype(BF16)
    else:
        p_ref[...] = (acc * cs_ref[...]).astype(BF16)


def _projection(x2d, mod, ng, w_in, colscale, layer, *, tm, mod_row, rope_tabs=None, kv_out=False, kv_prev=None,
                seq=None):
    m = x2d.shape[0]
    nm, nn = m // tm, IN_WIDTH // PROJ_TN
    rope = rope_tabs is not None

    def mod_spec(part):
        return pl.BlockSpec((None, None, 1, D_MODEL), lambda i, j: (layer, mod_row(i) * 3 + part, 0, 0))

    in_specs = [
        pl.BlockSpec((tm, D_MODEL), lambda i, j: (i, 0)),
        mod_spec(0), mod_spec(1),
        pl.BlockSpec((None, 1, D_MODEL), lambda i, j: (layer, 0, 0)),
        pl.BlockSpec((None, D_MODEL, PROJ_TN), lambda i, j: (layer, 0, j)),
        pl.BlockSpec((None, 1, PROJ_TN), lambda i, j: (j, 0, 0)),
    ]
    args = [x2d, mod, mod, ng, w_in, colscale]
    if rope:
        tiles_per_seq = seq // tm
        in_specs += [pl.BlockSpec((tm, LANES), lambda i, j: (i % tiles_per_seq, 0))] * 2
        args += list(rope_tabs)
    out_specs = [pl.BlockSpec((tm, PROJ_TN), lambda i, j: (i, j))]
    out_shape = [jax.ShapeDtypeStruct((m, IN_WIDTH), BF16)]
    aliases = {}
    if kv_out:
        nb, b = tm // 256, m // 256
        for heads, dh in ((NA_HEADS, NA_HEAD_DIM), (NA_HEADS, NA_HEAD_DIM),
                          (DIFF_HEADS, 2 * DIFF_QK_DIM), (DIFF_HEADS, DIFF_V_DIM)):
            if kv_prev is None:
                out_specs.append(pl.BlockSpec((nb, DEPTH, heads, 256, dh), lambda i, j: (i, 0, 0, 0, 0)))
            else:
                out_specs.append(pl.BlockSpec((nb, None, heads, 256, dh), lambda i, j: (i, layer, 0, 0, 0)))
            out_shape.append(jax.ShapeDtypeStruct((b, DEPTH, heads, 256, dh), F32))
        if kv_prev is not None:
            for k, a in enumerate(kv_prev):
                aliases[len(args)] = 1 + k
                in_specs.append(pl.BlockSpec(memory_space=pl.ANY))
                args.append(a)
    return pl.pallas_call(
        functools.partial(_proj_kernel, rope=rope, kv_out=kv_out, n_alias=len(aliases), tm=tm, layer=layer, nn=nn),
        grid=(nm, nn),
        in_specs=in_specs,
        out_specs=out_specs,
        out_shape=out_shape,
        input_output_aliases=aliases,
        scratch_shapes=[pltpu.VMEM((tm, D_MODEL), BF16)],
        compiler_params=_cparams(("arbitrary", "arbitrary")),
        name="in_proj_kv" if kv_out else "in_proj_rope",
    )(*args)


def _diff_lambda(lq1_ref, lk1_ref, lq2_ref, lk2_ref, lam_init):
    a = jnp.sum(lq1_ref[...] * lk1_ref[...], axis=-1, keepdims=True)
    b = jnp.sum(lq2_ref[...] * lk2_ref[...], axis=-1, keepdims=True)
    return jnp.exp(a) - jnp.exp(b) + lam_init


def _subln(o, g, lam_init):
    y = o * lax.rsqrt(jnp.mean(o * o, axis=-1, keepdims=True) + EPS)
    return (y * g) * (1.0 - lam_init)


def _prompt_attn_kernel(qa_ref, ka_ref, va_ref, qd_ref, kd_ref, vd_ref,
                        lq1_ref, lk1_ref, lq2_ref, lk2_ref, sg_ref, ona_ref, od_ref, *, lam_init):
    n = qa_ref.shape[0]
    mlo, mhi = _half_masks(BF16)
    for c in range(NA_HEADS // 2):
        sl = slice(c * LANES, (c + 1) * LANES)
        q, k, v = qa_ref[:, sl], ka_ref[:, sl], va_ref[:, sl]
        s = _dot_nt(jnp.concatenate([q * mlo, q * mhi], axis=0), k)
        e = jnp.exp(s - jnp.max(s, axis=-1, keepdims=True))
        p = (e / jnp.sum(e, axis=-1, keepdims=True)).astype(BF16)
        pcat = jnp.concatenate([p[:n], p[n:]], axis=1)
        vbd = jnp.concatenate([v * mlo, v * mhi], axis=0)
        ona_ref[:, sl] = _dot(pcat, vbd).astype(BF16)
    lam = _diff_lambda(lq1_ref, lk1_ref, lq2_ref, lk2_ref, lam_init)
    for h in range(DIFF_HEADS):
        sl = slice(h * LANES, (h + 1) * LANES)
        q, k, v = qd_ref[:, sl], kd_ref[:, sl], vd_ref[:, sl]
        s = _dot_nt(jnp.concatenate([q * mlo, q * mhi], axis=0), k)
        e = jnp.exp(s - jnp.max(s, axis=-1, keepdims=True))
        p = e / jnp.sum(e, axis=-1, keepdims=True)
        a = (p[:n] - lam * p[n:]).astype(BF16)
        od_ref[:, sl] = _subln(_dot(a, v), sg_ref[...], lam_init).astype(BF16)


def _prompt_attention(p, lam_params, sg, lam_init, layer, *, batch, seq):
    def cb(k):
        return pl.BlockSpec((seq, BRANCH_WIDTH), lambda b: (b, k))
    small = pl.BlockSpec((None, 1, DIFF_QK_DIM), lambda b: (layer, 0, 0))
    out = pl.BlockSpec((seq, BRANCH_WIDTH), lambda b: (b, 0))
    return pl.pallas_call(
        functools.partial(_prompt_attn_kernel, lam_init=lam_init),
        grid=(batch,),
        in_specs=[cb(CB_NA_Q), cb(CB_NA_K), cb(CB_NA_V), cb(CB_D_Q), cb(CB_D_K), cb(CB_D_V),
                  small, small, small, small, pl.BlockSpec((None, 1, DIFF_V_DIM), lambda b: (layer, 0, 0))],
        out_specs=[out, out],
        out_shape=[jax.ShapeDtypeStruct((batch * seq, BRANCH_WIDTH), BF16)] * 2,
        compiler_params=_cparams(("arbitrary",)),
        name="prompt_attn",
    )(p, p, p, p, p, p, *lam_params, sg)


def _na_variant_geometry(variant, rows):
    jv = (0, 1, rows // NA_QROWS - 1)[variant]
    kstart = min(max(NA_QROWS * jv - NA_WIN_ROWS // 2, 0), rows - NA_KROWS)
    return jv, kstart


def _na_build_bias(rpb_ref, bias_scr, layer, pair, rows):
    cq = lax.broadcasted_iota(jnp.int32, (GRID_W, GRID_W), 0)
    ck = lax.broadcasted_iota(jnp.int32, (GRID_W, GRID_W), 1)
    col_start = jnp.clip(cq - NA_WIN_COLS // 2, 0, GRID_W - NA_WIN_COLS)
    col_ok = (ck >= col_start) & (ck < col_start + NA_WIN_COLS)
    dc = jnp.clip(ck - cq + NA_WIN_COLS - 1, 0, 2 * NA_WIN_COLS - 2)
    n_dr, n_dc = 2 * NA_WIN_ROWS - 1, 2 * NA_WIN_COLS - 1
    neg = jnp.full((GRID_W, GRID_W), NEG_INF, F32)
    for hh in range(2):
        base = ((layer * NA_HEADS + 2 * pair + hh) * n_dr) * n_dc
        tabs = []
        for a in range(n_dr):
            t = jnp.zeros((GRID_W, GRID_W), F32)
            for b in range(n_dc):
                t = jnp.where(dc == b, rpb_ref[base + a * n_dc + b], t)
            tabs.append(jnp.where(col_ok, t, NEG_INF))
        for variant in range(3):
            jv, kstart = _na_variant_geometry(variant, rows)
            for i in range(NA_QROWS):
                r = NA_QROWS * jv + i
                rs = min(max(r - NA_WIN_ROWS // 2, 0), rows - NA_WIN_ROWS)
                blocks = []
                for u in range(NA_KROWS):
                    kr = kstart + u
                    blocks.append(tabs[kr - r + NA_WIN_ROWS - 1] if rs <= kr < rs + NA_WIN_ROWS else neg)
                bias_scr[hh, variant, i * GRID_W:(i + 1) * GRID_W, :] = jnp.concatenate(blocks, axis=1)


def _softmax_stage(s_scr, m_scr, e_scr, inv_scr, half, n_keys):
    wide = n_keys * SOFTMAX_EXP_ROWS // (8 * LANES) > VREGS
    if wide:
        for r0 in range(0, 2 * half, SOFTMAX_MAX_ROWS):
            m_scr[r0:r0 + SOFTMAX_MAX_ROWS, :] = jnp.max(s_scr[r0:r0 + SOFTMAX_MAX_ROWS, :], axis=-1, keepdims=True)
    for r0 in range(0, 2 * half, SOFTMAX_EXP_ROWS):
        rs = slice(r0, r0 + SOFTMAX_EXP_ROWS)
        s = s_scr[rs, :]
        e = jnp.exp(s - (m_scr[rs, :] if wide else jnp.max(s, axis=-1, keepdims=True)))
        inv_scr[rs, :] = 1.0 / jnp.sum(e, axis=-1, keepdims=True)
        if r0 < half:
            e_scr[rs, 0:n_keys] = e.astype(BF16)
        else:
            e_scr[r0 - half:r0 - half + SOFTMAX_EXP_ROWS, n_keys:2 * n_keys] = e.astype(BF16)


def _na_kernel(rpb_ref, q_ref, k_ref, kc_ref, v_ref, vc_ref, o_ref,
               bias_scr, s_ev, s_od, m_ev, m_od, e_ev, e_od, inv_ev, inv_od, *, layer, rows, n_steps):
    g = pl.program_id(0)
    nj = rows // NA_QROWS
    spp = nj // 2
    nq, nk = NA_QROWS * GRID_W, NA_KROWS * GRID_W
    n_keys = s_ev.shape[1]
    ga = jnp.minimum(g, n_steps - 1)
    gc = jnp.maximum(g - 1, 0)
    mlo, mhi = _half_masks(BF16)
    lane = lax.broadcasted_iota(jnp.int32, (1, LANES), 1)

    @pl.when(g == 0)
    def _():
        for r in (s_od, e_ev, e_od, inv_ev, inv_od):
            r[...] = jnp.zeros(r.shape, r.dtype)

    steps_per_pair = n_steps // (NA_HEADS // 2)

    @pl.when((g % steps_per_pair == 0) & (g < n_steps))
    def _():
        _na_build_bias(rpb_ref, bias_scr, layer, ga // steps_per_pair, rows)

    def key_start(j):
        kstart = jnp.clip(NA_QROWS * j - NA_WIN_ROWS // 2, 0, rows - NA_KROWS)
        return pl.multiple_of(kstart * GRID_W, 256)

    def stage_a(qrows, j, s_scr):
        variant = jnp.where(j == 0, 0, jnp.where(j == nj - 1, 2, 1))
        q = q_ref[qrows, :]
        qq = jnp.concatenate([q * mlo, q * mhi], axis=0)
        kl = k_ref[pl.ds(key_start(j), nk), :]
        bias = jnp.concatenate([bias_scr[0, variant], bias_scr[1, variant]], axis=0)
        s_scr[:, 0:nk] = _dot_nt(qq, kl) + bias
        s_scr[:, nk:n_keys] = _dot_nt(qq, kc_ref[...])

    def stage_c(orows, j, e_scr, inv_scr):
        vl, vc = v_ref[pl.ds(key_start(j), nk), :], vc_ref[...]
        vbd = jnp.concatenate([vl * mlo, vc * mlo, vl * mhi, vc * mhi], axis=0)
        o = _dot(e_scr[...], vbd)
        inv = inv_scr[...]
        o_ref[orows, :] = (o * jnp.where(lane < LANES // 2, inv[:nq], inv[nq:])).astype(BF16)

    ja, jc = 2 * (ga % spp), 2 * (gc % spp)
    first, second = slice(0, nq), slice(nq, 2 * nq)
    stage_a(first, ja, s_ev)
    _softmax_stage(s_od, m_od, e_od, inv_od, nq, n_keys)
    stage_c(first, jc, e_ev, inv_ev)
    stage_a(second, ja + 1, s_od)
    _softmax_stage(s_ev, m_ev, e_ev, inv_ev, nq, n_keys)
    stage_c(second, jc + 1, e_od, inv_od)


def _neighbourhood_attention(p, kc, vc, rpb_flat, layer, *, batch, seq):
    rows = seq // GRID_W
    nj = rows // NA_QROWS
    spp = nj // 2
    n_pairs = NA_HEADS // 2
    n_steps = n_pairs * batch * spp
    nq, nk = NA_QROWS * GRID_W, NA_KROWS * GRID_W
    n_ctx = kc.shape[2]
    n_keys = nk + n_ctx
    cpb = BRANCH_WIDTH // LANES

    def slot(gg):
        return gg // (batch * spp), (gg // spp) % batch, gg % spp

    def a_slot(g):
        return slot(jnp.minimum(g, n_steps - 1))

    def c_slot(g):
        return slot(jnp.maximum(g - 1, 0))

    def rows_map(sl, col0):
        def f(g):
            c, b, i = sl(g)
            return b * spp + i, col0 + c
        return f

    def seq_map(sl, col0):
        def f(g):
            c, b, _ = sl(g)
            return b, col0 + c
        return f

    def ctx_map(sl):
        def f(g):
            c, b, _ = sl(g)
            return b, layer, 0, c
        return f

    return pl.pallas_call(
        functools.partial(_na_kernel, layer=layer, rows=rows, n_steps=n_steps),
        grid=(n_steps + 1,),
        in_specs=[
            pl.BlockSpec(memory_space=pltpu.SMEM),
            pl.BlockSpec((2 * nq, LANES), rows_map(a_slot, CB_NA_Q * cpb)),
            pl.BlockSpec((seq, LANES), seq_map(a_slot, CB_NA_K * cpb)),
            pl.BlockSpec((None, None, n_ctx, LANES), ctx_map(a_slot)),
            pl.BlockSpec((seq, LANES), seq_map(c_slot, CB_NA_V * cpb)),
            pl.BlockSpec((None, None, n_ctx, LANES), ctx_map(c_slot)),
        ],
        out_specs=pl.BlockSpec((2 * nq, LANES), rows_map(c_slot, 0)),
        out_shape=jax.ShapeDtypeStruct((batch * seq, BRANCH_WIDTH), BF16),
        scratch_shapes=[
            pltpu.VMEM((2, 3, nq, nk), F32),
            pltpu.VMEM((2 * nq, n_keys), F32), pltpu.VMEM((2 * nq, n_keys), F32),
            pltpu.VMEM((2 * nq, 1), F32), pltpu.VMEM((2 * nq, 1), F32),
            pltpu.VMEM((nq, 2 * n_keys), BF16), pltpu.VMEM((nq, 2 * n_keys), BF16),
            pltpu.VMEM((2 * nq, 1), F32), pltpu.VMEM((2 * nq, 1), F32),
        ],
        compiler_params=_cparams(("arbitrary",)),
        name="na_attn",
    )(rpb_flat, p, p, kc, p, vc)


def _diff_kernel(q_ref, k_ref, v_ref, kc_ref, vc_ref, lq1_ref, lk1_ref, lq2_ref, lk2_ref, sg_ref, o_ref,
                 kall_scr, vbd_scr, s_ev, s_od, m_ev, m_od, e_ev, e_od, inv_ev, inv_od,
                 *, lam_init, seq, steps_per_head, n_steps):
    g = pl.program_id(0)
    tq = DIFF_TQ
    n_all = kall_scr.shape[0]
    mlo, mhi = _half_masks(BF16)
    head = jnp.minimum(g, n_steps - 1) // steps_per_head
    prev_head = jnp.maximum(g - 1, 0) // steps_per_head

    @pl.when(g == 0)
    def _():
        for r in (s_od, m_od, e_ev, e_od, inv_ev, inv_od):
            r[...] = jnp.zeros(r.shape, r.dtype)

    @pl.when((g % steps_per_head == 0) & (g < n_steps))
    def _():
        kall_scr[0:seq, :] = k_ref[...]
        kall_scr[seq:n_all, :] = kc_ref[...].astype(BF16)
        v_all = jnp.concatenate([v_ref[...], vc_ref[...].astype(BF16)], axis=0)
        z = jnp.zeros_like(v_all)
        vbd_scr[head % 2, 0:n_all, :] = jnp.concatenate([v_all, z], axis=1)
        vbd_scr[head % 2, n_all:2 * n_all, :] = jnp.concatenate([z, v_all], axis=1)

    lam = _diff_lambda(lq1_ref, lk1_ref, lq2_ref, lk2_ref, lam_init)

    def stage_a(rows, s_scr, m_scr):
        q = q_ref[rows, :]
        s_scr[...] = _dot_nt(jnp.concatenate([q * mlo, q * mhi], axis=0), kall_scr[...])

    def stage_b(s_scr, m_scr, e_scr, inv_scr):
        _softmax_stage(s_scr, m_scr, e_scr, inv_scr, tq, n_all)

    def stage_c(rows, e_scr, inv_scr):
        o12 = _dot(e_scr[...], vbd_scr[prev_head % 2])
        inv = inv_scr[...]
        o = o12[:, :DIFF_V_DIM] * inv[:tq] - lam * (o12[:, DIFF_V_DIM:] * inv[tq:])
        o_ref[rows, :] = _subln(o, sg_ref[...], lam_init).astype(BF16)

    first, second = slice(0, tq), slice(tq, 2 * tq)
    stage_a(first, s_ev, m_ev)
    stage_b(s_od, m_od, e_od, inv_od)
    stage_c(first, e_ev, inv_ev)
    stage_a(second, s_od, m_od)
    stage_b(s_ev, m_ev, e_ev, inv_ev)
    stage_c(second, e_od, inv_od)


def _diff_attention(p, kc, vc, lam_params, sg, lam_init, layer, *, batch, seq):
    nq = seq // DIFF_TQ
    sph = nq // 2
    n_steps = batch * DIFF_HEADS * sph
    n_ctx = kc.shape[3]
    n_all = seq + n_ctx
    cpb = BRANCH_WIDTH // LANES
    tq2 = 2 * DIFF_TQ

    def a_slot(g):
        hd = jnp.minimum(g, n_steps - 1) // sph
        return hd // DIFF_HEADS, hd % DIFF_HEADS, jnp.minimum(g, n_steps - 1) % sph

    def c_slot(g):
        gg = jnp.maximum(g - 1, 0)
        hd = gg // sph
        return hd // DIFF_HEADS, hd % DIFF_HEADS, gg % sph

    def q_map(g):
        b, h, i = a_slot(g)
        return b * sph + i, CB_D_Q * cpb + h

    def o_map(g):
        b, h, i = c_slot(g)
        return b * sph + i, h

    small = pl.BlockSpec((None, 1, DIFF_QK_DIM), lambda g: (layer, 0, 0))
    ctx = pl.BlockSpec((None, None, None, n_ctx, LANES), lambda g: (a_slot(g)[0], layer, a_slot(g)[1], 0, 0))
    return pl.pallas_call(
        functools.partial(_diff_kernel, lam_init=lam_init, seq=seq, steps_per_head=sph, n_steps=n_steps),
        grid=(n_steps + 1,),
        in_specs=[
            pl.BlockSpec((tq2, LANES), q_map),
            pl.BlockSpec((seq, LANES), lambda g: (a_slot(g)[0], CB_D_K * cpb + a_slot(g)[1])),
            pl.BlockSpec((seq, LANES), lambda g: (a_slot(g)[0], CB_D_V * cpb + a_slot(g)[1])),
            ctx, ctx, small, small, small, small,
            pl.BlockSpec((None, 1, DIFF_V_DIM), lambda g: (layer, 0, 0)),
        ],
        out_specs=pl.BlockSpec((tq2, LANES), o_map),
        out_shape=jax.ShapeDtypeStruct((batch * seq, BRANCH_WIDTH), BF16),
        scratch_shapes=[
            pltpu.VMEM((n_all, LANES), BF16), pltpu.VMEM((2, 2 * n_all, 2 * LANES), BF16),
            pltpu.VMEM((tq2, n_all), F32), pltpu.VMEM((tq2, n_all), F32),
            pltpu.VMEM((tq2, 1), F32), pltpu.VMEM((tq2, 1), F32),
            pltpu.VMEM((DIFF_TQ, 2 * n_all), BF16), pltpu.VMEM((DIFF_TQ, 2 * n_all), BF16),
            pltpu.VMEM((tq2, 1), F32), pltpu.VMEM((tq2, 1), F32),
        ],
        compiler_params=_cparams(("arbitrary",)),
        name="diff_attn",
    )(p, p, p, kc, vc, *lam_params, sg)


def _fnet_tail(x3, cs_ref, wf_ref, n):
    f = _dot(x3[:n].astype(BF16), cs_ref[0:BRANCH_WIDTH, :]) + _dot(x3[n:].astype(BF16), cs_ref[BRANCH_WIDTH:, :])
    return _dot(f.astype(BF16), wf_ref[...])


def _fnet_small_kernel(u_ref, dft_ref, cs_ref, wf_ref, o_ref):
    n = u_ref.shape[0]
    x3 = _dot(dft_ref[...], u_ref[...])
    o_ref[...] = _fnet_tail(x3, cs_ref, wf_ref, n).astype(BF16)


def _fnet_small(p, dft, cs, wf, layer, *, batch, seq):
    return pl.pallas_call(
        _fnet_small_kernel,
        grid=(batch,),
        in_specs=[
            pl.BlockSpec((seq, BRANCH_WIDTH), lambda b: (b, CB_F_U)),
            pl.BlockSpec(dft.shape, lambda b: (0, 0)),
            pl.BlockSpec(cs.shape, lambda b: (0, 0)),
            pl.BlockSpec((None,) + wf.shape[1:], lambda b: (layer, 0, 0)),
        ],
        out_specs=pl.BlockSpec((seq, BRANCH_WIDTH), lambda b: (b, 0)),
        out_shape=jax.ShapeDtypeStruct((batch * seq, BRANCH_WIDTH), BF16),
        compiler_params=_cparams(("arbitrary",)),
        name="fnet_prompt",
    )(p, dft, cs, wf)


FFT_CHUNK = 16


def _fft_stage1(u_ref, twc_ref, tws_ref, s_scr, n2):
    h = math.sqrt(0.5)
    rep = BRANCH_WIDTH // LANES

    def body(ci, carry):
        r0 = pl.multiple_of(ci * FFT_CHUNK, FFT_CHUNK)
        a = [u_ref[pl.ds(r * n2 + r0, FFT_CHUNK), :].astype(F32) for r in range(FFT_RADIX)]
        s04, d04, s26, d26 = a[0] + a[4], a[0] - a[4], a[2] + a[6], a[2] - a[6]
        s15, d15, s37, d37 = a[1] + a[5], a[1] - a[5], a[3] + a[7], a[3] - a[7]
        ee, oo = s04 + s26, s15 + s37
        dm, dp = h * (d15 - d37), h * (d15 + d37)
        zero = jnp.zeros_like(ee)
        re = [ee + oo, d04 + dm, s04 - s26, d04 - dm, ee - oo]
        im = [zero, dp + d26, s15 - s37, dp - d26, zero]
        for k1 in range(FFT_RADIX):
            kk = k1 if k1 <= FFT_RADIX // 2 else FFT_RADIX - k1
            ar, ai = re[kk], (im[kk] if k1 <= FFT_RADIX // 2 else -im[kk])
            ct = jnp.tile(twc_ref[k1, pl.ds(r0, FFT_CHUNK), :], (1, rep))
            st = jnp.tile(tws_ref[k1, pl.ds(r0, FFT_CHUNK), :], (1, rep))
            s_scr[k1, pl.ds(r0, FFT_CHUNK), :] = (ar * ct - ai * st).astype(BF16)
            s_scr[k1, pl.ds(n2 + r0, FFT_CHUNK), :] = (ar * st + ai * ct).astype(BF16)
        return carry

    lax.fori_loop(0, n2 // FFT_CHUNK, body, 0)


def _fnet_large_kernel(u_ref, twc_ref, tws_ref, d3_ref, cs_ref, wf_ref, o_ref, s_scr):
    k1 = pl.program_id(1)
    n2 = s_scr.shape[1] // 2

    @pl.when(k1 == 0)
    def _():
        _fft_stage1(u_ref, twc_ref, tws_ref, s_scr, n2)

    x3 = _dot(d3_ref[...], s_scr[k1])
    o_ref[...] = _fnet_tail(x3, cs_ref, wf_ref, n2).astype(BF16)


def _fnet_large(p, twc, tws, d3, cs, wf, layer, *, batch, seq):
    n2 = seq // FFT_RADIX
    out = pl.pallas_call(
        _fnet_large_kernel,
        grid=(batch, FFT_RADIX),
        in_specs=[
            pl.BlockSpec((seq, BRANCH_WIDTH), lambda b, k: (b, CB_F_U)),
            pl.BlockSpec(twc.shape, lambda b, k: (0, 0, 0)),
            pl.BlockSpec(tws.shape, lambda b, k: (0, 0, 0)),
            pl.BlockSpec(d3.shape, lambda b, k: (0, 0)),
            pl.BlockSpec(cs.shape, lambda b, k: (0, 0)),
            pl.BlockSpec((None,) + wf.shape[1:], lambda b, k: (layer, 0, 0)),
        ],
        out_specs=pl.BlockSpec((None, n2, BRANCH_WIDTH), lambda b, k: (b, 0, k)),
        out_shape=jax.ShapeDtypeStruct((batch, n2, FFT_RADIX * BRANCH_WIDTH), BF16),
        scratch_shapes=[pltpu.VMEM((FFT_RADIX, 2 * n2, BRANCH_WIDTH), BF16)],
        compiler_params=_cparams(("arbitrary", "arbitrary")),
        name="fnet_sample",
    )(p, twc, tws, d3, cs, wf)
    return out.reshape(batch * seq, BRANCH_WIDTH)


def _dft_constants(seq):
    gd = FNET_GROUP_DIM
    norm = 1.0 / math.sqrt(seq * gd)
    ang = 2.0 * np.pi * (np.outer(np.arange(gd), np.arange(gd)) % gd) / gd
    eye = np.eye(FNET_GROUPS)
    cs = np.concatenate([np.kron(eye, np.cos(ang)), -np.kron(eye, np.sin(ang))], axis=0) * norm
    if seq % (FFT_RADIX * FFT_CHUNK) != 0 or seq < 2048:
        ang_t = 2.0 * np.pi * (np.outer(np.arange(seq), np.arange(seq)) % seq) / seq
        dft = np.concatenate([np.cos(ang_t), np.sin(ang_t)], axis=0)
        return dict(dft=jnp.asarray(dft, F32).astype(BF16), cs=jnp.asarray(cs, F32).astype(BF16))
    n2 = seq // FFT_RADIX
    ang2 = 2.0 * np.pi * (np.outer(np.arange(n2), np.arange(n2)) % n2) / n2
    c2, s2 = np.cos(ang2), np.sin(ang2)
    d3 = np.block([[c2, -s2], [s2, c2]])
    angw = 2.0 * np.pi * np.outer(np.arange(FFT_RADIX), np.arange(n2)) / seq
    twc = np.repeat(np.cos(angw)[:, :, None], LANES, axis=2)
    tws = np.repeat(np.sin(angw)[:, :, None], LANES, axis=2)
    return dict(d3=jnp.asarray(d3, F32).astype(BF16), cs=jnp.asarray(cs, F32).astype(BF16),
                twc=jnp.asarray(twc, F32), tws=jnp.asarray(tws, F32))


def _merge_kernel(x_ref, ona_ref, of_ref, od_ref, zna_ref, zf_ref, zd_ref, g0_ref, g1_ref, g2_ref,
                  gate_ref, wb_ref, wo_ref, fg_ref, o_ref, *, final):
    y = None
    for b, (o_b, z_b, g_b) in enumerate(((ona_ref, zna_ref, g0_ref), (of_ref, zf_ref, g1_ref), (od_ref, zd_ref, g2_ref))):
        z = z_b[...].astype(F32)
        u = (o_b[...].astype(F32) * (z * jax.nn.sigmoid(z))).astype(BF16)
        t = jax.nn.sigmoid(g_b[...].astype(F32)) * _dot(u, wb_ref[b])
        y = t if y is None else y + t
    xn = x_ref[...] + gate_ref[...] * _dot(y.astype(BF16), wo_ref[...])
    if final:
        xn = xn * lax.rsqrt(jnp.mean(xn * xn, axis=-1, keepdims=True) + EPS) * fg_ref[...]
    o_ref[...] = xn


def _merge(x2d, o_na, o_f, o_d, p, mod, wb, wo, fg, layer, *, tm, mod_row, final):
    m = x2d.shape[0]

    def cb(k):
        return pl.BlockSpec((tm, BRANCH_WIDTH), lambda i: (i, k))

    def gb(k):
        return pl.BlockSpec((tm, D_MODEL), lambda i: (i, CB_G // 2 + k))

    br = pl.BlockSpec((tm, BRANCH_WIDTH), lambda i: (i, 0))
    return pl.pallas_call(
        functools.partial(_merge_kernel, final=final),
        grid=(m // tm,),
        in_specs=[
            pl.BlockSpec((tm, D_MODEL), lambda i: (i, 0)), br, br, br,
            cb(CB_NA_Z), cb(CB_F_Z), cb(CB_D_Z), gb(0), gb(1), gb(2),
            pl.BlockSpec((None, None, 1, D_MODEL), lambda i: (layer, mod_row(i) * 3 + 2, 0, 0)),
            pl.BlockSpec((None,) + wb.shape[1:], lambda i: (layer, 0, 0, 0)),
            pl.BlockSpec((None,) + wo.shape[1:], lambda i: (layer, 0, 0)),
            pl.BlockSpec((1, D_MODEL), lambda i: (0, 0)),
        ],
        out_specs=pl.BlockSpec((tm, D_MODEL), lambda i: (i, 0)),
        out_shape=jax.ShapeDtypeStruct((m, D_MODEL), F32),
        compiler_params=_cparams(("arbitrary",)),
        name="merge_final" if final else "merge",
    )(x2d, o_na, o_f, o_d, p, p, p, p, p, p, mod, wb, wo, fg)


def _rope_tables(seq):
    t = np.arange(seq)
    row = (t // GRID_W).astype(np.float64)
    col = (t % GRID_W).astype(np.float64)
    half = DIFF_QK_DIM // 2
    inv = ROPE_BASE ** (-np.arange(0, half, 2, dtype=np.float64) / half)
    ar, ac = row[:, None] * inv, col[:, None] * inv
    cr, sr, cc, sc = np.cos(ar), np.sin(ar), np.cos(ac), np.sin(ac)
    cos = np.concatenate([cr, cr, cc, cc] * 2, axis=-1)
    sin = np.concatenate([-sr, sr, -sc, sc] * 2, axis=-1)
    return jnp.asarray(cos, F32), jnp.asarray(sin, F32)


def _column_scales():
    s = np.ones((IN_WIDTH,), np.float32)
    s[CB_NA_Q * 512:(CB_NA_Q + 1) * 512] = NA_HEAD_DIM ** -0.5
    s[CB_D_Q * 512:(CB_D_Q + 1) * 512] = DIFF_QK_DIM ** -0.5
    return jnp.asarray(s.reshape(IN_WIDTH // PROJ_TN, 1, PROJ_TN))


def kernel(x_prompt, x_sample, cache_na_k, cache_na_v, cache_diff_k, cache_diff_v, c, c_ctx, norm_g, w_mod, b_mod,
           w_in, na_rpb, fnet_w, diff_lq1, diff_lk1, diff_lq2, diff_lk2, diff_subln_g, w_branch, w_out, final_g):
    bp, n = x_prompt.shape[:2]
    bs, t = x_sample.shape[:2]
    assert n == 256 and t % (NA_KROWS * GRID_W) == 0 and 1 + bs <= 8

    w_in_b, w_mod_b = w_in.astype(BF16), w_mod.astype(BF16)
    wb_b, wo_b, wf_b = w_branch.astype(BF16), w_out.astype(BF16), fnet_w.astype(BF16)

    cvecs = jnp.concatenate([c_ctx[None], c, jnp.zeros((8 - 1 - bs, D_MODEL), F32)], axis=0)
    mod = _modulation(cvecs, w_mod_b, b_mod).reshape(DEPTH, 8 * 3, 1, D_MODEL)

    colscale = _column_scales()
    rope_tabs = _rope_tables(t)
    fft_p, fft_s = _dft_constants(n), _dft_constants(t)
    na_kc = cache_na_k.transpose(0, 1, 3, 2, 4).reshape(bs, DEPTH, -1, NA_HEADS * NA_HEAD_DIM).astype(BF16)
    na_vc = cache_na_v.transpose(0, 1, 3, 2, 4).reshape(bs, DEPTH, -1, NA_HEADS * NA_HEAD_DIM).astype(BF16)
    rpb_flat = na_rpb.reshape(-1)
    fg = final_g.reshape(1, D_MODEL)

    tm_p, tm_s = 512, 1024
    tiles_per_seq = t // tm_s
    row_p = lambda i: 0
    row_s = lambda i: 1 + i // tiles_per_seq

    xp = x_prompt.reshape(bp * n, D_MODEL)
    xs = x_sample.reshape(bs * t, D_MODEL)
    lam_params = [a.reshape(DEPTH, 1, DIFF_QK_DIM) for a in (diff_lq1, diff_lk1, diff_lq2, diff_lk2)]
    sg = diff_subln_g.reshape(DEPTH, 1, DIFF_V_DIM)
    ng = norm_g.reshape(DEPTH, 1, D_MODEL)
    kv = None
    for l in range(DEPTH):
        lam_init = 0.8 - 0.6 * math.exp(-0.3 * l)
        final = l == DEPTH - 1

        p, *kv = _projection(xp, mod, ng, w_in_b, colscale, l, tm=tm_p, mod_row=row_p, kv_out=True, kv_prev=kv)
        o_na, o_d = _prompt_attention(p, lam_params, sg, lam_init, l, batch=bp, seq=n)
        o_f = _fnet_small(p, fft_p["dft"], fft_p["cs"], wf_b, l, batch=bp, seq=n)
        xp = _merge(xp, o_na, o_f, o_d, p, mod, wb_b, wo_b, fg, l, tm=tm_p, mod_row=row_p, final=final)

        (p,) = _projection(xs, mod, ng, w_in_b, colscale, l, tm=tm_s, mod_row=row_s, rope_tabs=rope_tabs, seq=t)
        o_na = _neighbourhood_attention(p, na_kc, na_vc, rpb_flat, l, batch=bs, seq=t)
        o_d = _diff_attention(p, cache_diff_k, cache_diff_v, lam_params, sg, lam_init, l, batch=bs, seq=t)
        o_f = _fnet_large(p, fft_s["twc"], fft_s["tws"], fft_s["d3"], fft_s["cs"], wf_b, l, batch=bs, seq=t)
        xs = _merge(xs, o_na, o_f, o_d, p, mod, wb_b, wo_b, fg, l, tm=tm_s, mod_row=row_s, final=final)

    y_prompt = xp.reshape(bp, n, D_MODEL)
    y_sample = xs.reshape(bs, t, D_MODEL)
    new_na_k, new_na_v, new_diff_k, new_diff_v = kv
    return (y_prompt, y_sample, new_na_k, new_na_v, new_diff_k, new_diff_v)
```
